```python
import math
import jax, jax.numpy as jnp
from jax import lax
import numpy as np

D_MODEL = 1024
BATCH = 2
SEQ = 8192
DEPTH = 2

N_MIXERS = 2
N_LRU_LAYERS = (DEPTH + 1) // 2
N_FOX_LAYERS = DEPTH // 2
EPS = 1e-6
LRU_WIDTH = 1536
LRU_BLOCKS = 12
LRU_BLOCK_W = LRU_WIDTH // LRU_BLOCKS
CONV_WIDTH = 4
LRU_C = 8.0
FOX_HEADS = 16
FOX_HEAD_DIM = 64
FOX_WIDTH = FOX_HEADS * FOX_HEAD_DIM
Q_BLOCK = 128
NEG_INF = -1e30

kernel_name = "hybrid_rglru_fox_interleaved"


def rms_norm(x, g):
    xf = x.astype(jnp.float32)
    y = xf * lax.rsqrt(jnp.mean(xf * xf, axis=-1, keepdims=True) + EPS)
    return (y * g.astype(jnp.float32)).astype(x.dtype)


def causal_depthwise_conv(x, w, b):
    c = x.shape[-1]
    y = lax.conv_general_dilated(
        x, w[:, None, :].astype(x.dtype), window_strides=(1,),
        padding=[(CONV_WIDTH - 1, 0)],
        dimension_numbers=("NWC", "WIO", "NWC"),
        feature_group_count=c)
    return y + b.astype(x.dtype)


def block_diag_linear(x, w, b):
    bsz, s, _ = x.shape
    xb = x.reshape(bsz, s, LRU_BLOCKS, LRU_BLOCK_W)
    y = jnp.einsum("bsnc,ncd->bsnd", xb, w.astype(x.dtype))
    return y.reshape(bsz, s, LRU_WIDTH) + b.astype(x.dtype)


def lru_mixer(h, w_in, conv_w, conv_b, wa, ba, wx, bx, a_param, w_out):
    u = h @ w_in.astype(h.dtype)
    xb, gate = u[..., :LRU_WIDTH], u[..., LRU_WIDTH:]
    xc = causal_depthwise_conv(xb, conv_w, conv_b)
    r = jax.nn.sigmoid(block_diag_linear(xc, wa, ba).astype(jnp.float32))
    i = jax.nn.sigmoid(block_diag_linear(xc, wx, bx).astype(jnp.float32))
    log_a = -LRU_C * r * jax.nn.softplus(-a_param.astype(jnp.float32))
    a = jnp.exp(log_a)
    mult = jnp.sqrt(-jnp.expm1(2.0 * log_a))
    bterm = mult * (i * xc.astype(jnp.float32))

    def combine(lhs, rhs):
        a1, b1 = lhs
        a2, b2 = rhs
        return a1 * a2, a2 * b1 + b2

    _, hs = lax.associative_scan(combine, (a, bterm), axis=1)
    y = hs.astype(h.dtype) * jax.nn.silu(gate)
    return y @ w_out.astype(h.dtype)


def fox_mixer(h, w_in, b_f, w_out):
    bsz, s, _ = h.shape
    u = h @ w_in.astype(h.dtype)
    q = u[..., 0 * FOX_WIDTH:1 * FOX_WIDTH].reshape(bsz, s, FOX_HEADS, FOX_HEAD_DIM)
    k = u[..., 1 * FOX_WIDTH:2 * FOX_WIDTH].reshape(bsz, s, FOX_HEADS, FOX_HEAD_DIM)
    v = u[..., 2 * FOX_WIDTH:3 * FOX_WIDTH].reshape(bsz, s, FOX_HEADS, FOX_HEAD_DIM)
    gate = u[..., 3 * FOX_WIDTH:4 * FOX_WIDTH]
    f_logit = u[..., 4 * FOX_WIDTH:].astype(jnp.float32) + b_f.astype(jnp.float32)
    cum = jnp.cumsum(jax.nn.log_sigmoid(f_logit), axis=1)
    ck = jnp.transpose(cum, (0, 2, 1))
    scale = 1.0 / math.sqrt(FOX_HEAD_DIM)
    n_blocks = s // Q_BLOCK
    qb = jnp.transpose(q.reshape(bsz, n_blocks, Q_BLOCK, FOX_HEADS, FOX_HEAD_DIM), (1, 0, 2, 3, 4))
    cqb = jnp.transpose(cum.reshape(bsz, n_blocks, Q_BLOCK, FOX_HEADS), (1, 0, 3, 2))
    starts = jnp.arange(n_blocks, dtype=jnp.int32) * Q_BLOCK
    kpos = jnp.arange(s, dtype=jnp.int32)
    kf = k.astype(jnp.float32)
    vf = v.astype(jnp.float32)

    def one_block(args):
        q_blk, cq_blk, start = args
        qpos = start + jnp.arange(Q_BLOCK, dtype=jnp.int32)
        logits = jnp.einsum("bqhd,bkhd->bhqk", q_blk.astype(jnp.float32), kf) * scale
        logits = logits + (cq_blk[..., :, None] - ck[:, :, None, :])
        mask = kpos[None, :] <= qpos[:, None]
        logits = jnp.where(mask[None, None], logits, NEG_INF)
        p = jax.nn.softmax(logits, axis=-1)
        return jnp.einsum("bhqk,bkhd->bqhd", p, vf)

    o = lax.map(one_block, (qb, cqb, starts))
    o = jnp.transpose(o, (1, 0, 2, 3, 4)).reshape(bsz, s, FOX_WIDTH).astype(h.dtype)
    y = o * jax.nn.silu(gate)
    return y @ w_out.astype(h.dtype)


def setup_inputs(seed: int = 0) -> dict:
    key = jax.random.key(seed)
    ks = jax.random.split(key, 16)
    f32 = jnp.float32
    nl, nf = N_LRU_LAYERS, N_FOX_LAYERS
    x = jax.random.normal(ks[0], (BATCH, SEQ, D_MODEL), f32)
    norm_g = 1.0 + 0.05 * jax.random.normal(ks[1], (DEPTH, D_MODEL), f32)
    final_g = 1.0 + 0.05 * jax.random.normal(ks[2], (D_MODEL,), f32)
    lru_w_in = jax.random.normal(ks[3], (nl, D_MODEL, 2 * LRU_WIDTH), f32) * D_MODEL ** -0.5
    lru_conv_w = jax.random.normal(ks[4], (nl, CONV_WIDTH, LRU_WIDTH), f32) * CONV_WIDTH ** -0.5
    lru_conv_b = 0.02 * jax.random.normal(ks[5], (nl, LRU_WIDTH), f32)
    lru_wa = jax.random.normal(ks[6], (nl, LRU_BLOCKS, LRU_BLOCK_W, LRU_BLOCK_W), f32) * LRU_BLOCK_W ** -0.5
    lru_ba = 0.02 * jax.random.normal(ks[7], (nl, LRU_WIDTH), f32)
    lru_wx = jax.random.normal(ks[8], (nl, LRU_BLOCKS, LRU_BLOCK_W, LRU_BLOCK_W), f32) * LRU_BLOCK_W ** -0.5
    lru_bx = 0.02 * jax.random.normal(ks[9], (nl, LRU_WIDTH), f32)
    a_c = jax.random.uniform(ks[10], (nl, LRU_WIDTH), f32, minval=0.9, maxval=0.999)
    a0 = a_c ** (1.0 / LRU_C)
    lru_a_param = jnp.log(a0) - jnp.log1p(-a0)
    lru_w_out = jax.random.normal(ks[11], (nl, LRU_WIDTH, D_MODEL), f32) * LRU_WIDTH ** -0.5
    fox_w_in = jax.random.normal(ks[12], (nf, D_MODEL, 4 * FOX_WIDTH + FOX_HEADS), f32) * D_MODEL ** -0.5
    fox_b_f = 3.0 + 0.5 * jax.random.normal(ks[13], (nf, FOX_HEADS), f32)
    fox_w_out = jax.random.normal(ks[14], (nf, FOX_WIDTH, D_MODEL), f32) * FOX_WIDTH ** -0.5
    return {"x": x, "norm_g": norm_g, "final_g": final_g,
            "lru_w_in": lru_w_in, "lru_conv_w": lru_conv_w, "lru_conv_b": lru_conv_b,
            "lru_wa": lru_wa, "lru_ba": lru_ba, "lru_wx": lru_wx, "lru_bx": lru_bx,
            "lru_a_param": lru_a_param, "lru_w_out": lru_w_out,
            "fox_w_in": fox_w_in, "fox_b_f": fox_b_f, "fox_w_out": fox_w_out}


def reference(x, norm_g, final_g, lru_w_in, lru_conv_w, lru_conv_b, lru_wa, lru_ba,
              lru_wx, lru_bx, lru_a_param, lru_w_out, fox_w_in, fox_b_f, fox_w_out):
    for i in range(DEPTH):
        h = rms_norm(x, norm_g[i])
        j = i // N_MIXERS
        if i % N_MIXERS == 0:
            x = x + lru_mixer(h, lru_w_in[j], lru_conv_w[j], lru_conv_b[j], lru_wa[j], lru_ba[j],
                              lru_wx[j], lru_bx[j], lru_a_param[j], lru_w_out[j])
        else:
            x = x + fox_mixer(h, fox_w_in[j], fox_b_f[j], fox_w_out[j])
    return rms_norm(x, final_g)
```

```python
import functools

import jax
import jax.numpy as jnp
from jax import lax
from jax.experimental import pallas as pl
from jax.experimental.pallas import tpu as pltpu

F32 = jnp.float32
BF16 = jnp.bfloat16

EPS = 1e-6
LRU_C = 8.0
CONV_WIDTH = 4
NEG_INF = -1e30

LANES = 128
SUBLANES = 8
VMEM_LIMIT_BYTES = 56 * 1024 * 1024

LRU_TILE = 256
FOX_TILE = 512
ATTN_TILE = 512

C_LANE = 112


def _rmsnorm(x, g):
    return x * lax.rsqrt(jnp.mean(x * x, axis=-1, keepdims=True) + EPS) * g


def _sigmoid(x):
    return 1.0 / (1.0 + jnp.exp(-x))


def _log_sigmoid(x):
    return jnp.minimum(x, 0.0) - jnp.log1p(jnp.exp(-jnp.abs(x)))


def _split3(c):
    c1 = c.astype(BF16)
    r1 = c - c1.astype(F32)
    c2 = r1.astype(BF16)
    c3 = (r1 - c2.astype(F32)).astype(BF16)
    return c1, c2, c3


def _dot(a, b):
    return jnp.dot(a, b, preferred_element_type=F32)


def _dot_nt(a, b):
    return lax.dot_general(a, b, (((1,), (1,)), ((), ())), preferred_element_type=F32)


def _lru_kernel(x_ref, g_ref, win_ref, cw_ref, cb_ref, wg_ref, bg_ref, ap_ref, wout_ref, o_ref,
                ext_ref, a_ref, b_ref, hs_ref, hcar_ref, *, tile, width, n_blocks):
    t = pl.program_id(1)
    blk = width // n_blocks

    @pl.when(t == 0)
    def _():
        ext_ref[0:SUBLANES, :] = jnp.zeros((SUBLANES, width), F32)
        hcar_ref[...] = jnp.zeros_like(hcar_ref)

    x = x_ref[...]
    xn = _rmsnorm(x, g_ref[...]).astype(BF16)
    u = _dot(xn, win_ref[...])
    gate = u[:, width:]
    ext_ref[SUBLANES:SUBLANES + tile, :] = u[:, :width]

    cw = cw_ref[...]
    xc = cb_ref[...] + cw[3:4] * ext_ref[SUBLANES:SUBLANES + tile, :]
    for k in range(CONV_WIDTH - 1):
        off = SUBLANES - (CONV_WIDTH - 1) + k
        xc = xc + cw[k:k + 1] * ext_ref[off:off + tile, :]
    ext_ref[0:SUBLANES, :] = ext_ref[tile:tile + SUBLANES, :]

    ap = ap_ref[...]
    sp = jnp.maximum(-ap, 0.0) + jnp.log1p(jnp.exp(-jnp.abs(ap)))
    xcb = xc.astype(BF16)
    for n in range(n_blocks):
        sl = slice(n * blk, (n + 1) * blk)
        g = _dot(xcb[:, sl], wg_ref[n]) + bg_ref[n]
        r = _sigmoid(g[:, :blk])
        i = _sigmoid(g[:, blk:])
        neg_log_a = LRU_C * r * sp[:, sl]
        a = jnp.exp(-neg_log_a)
        a_ref[:, sl] = a
        b_ref[:, sl] = jnp.sqrt(jnp.tanh(neg_log_a) * (1.0 + a * a)) * (i * xc[:, sl])

    def step(i, h):
        h = a_ref[pl.ds(i, 1), :] * h + b_ref[pl.ds(i, 1), :]
        hs_ref[pl.ds(i, 1), :] = h
        return h

    h = lax.fori_loop(0, tile, step, hcar_ref[0:1, :], unroll=SUBLANES)
    hcar_ref[0:1, :] = h

    y = hs_ref[...] * (gate * _sigmoid(gate))
    o_ref[...] = x + _dot(y.astype(BF16), wout_ref[...])


def _lru_layer(x, g, w_in, conv_w, conv_b, wg, bg, a_param, w_out, n_blocks):
    bsz, s, d = x.shape
    width = w_out.shape[0]
    tile = LRU_TILE
    kern = functools.partial(_lru_kernel, tile=tile, width=width, n_blocks=n_blocks)
    const = lambda *shape: pl.BlockSpec(shape, lambda b, t: (0,) * len(shape))
    return pl.pallas_call(
        kern,
        grid=(bsz, s // tile),
        in_specs=[
            pl.BlockSpec((None, tile, d), lambda b, t: (b, t, 0)),
            const(1, d),
            const(d, 2 * width),
            const(CONV_WIDTH, width),
            const(1, width),
            const(n_blocks, width // n_blocks, 2 * (width // n_blocks)),
            const(n_blocks, 1, 2 * (width // n_blocks)),
            const(1, width),
            const(width, d),
        ],
        out_specs=pl.BlockSpec((None, tile, d), lambda b, t: (b, t, 0)),
        out_shape=jax.ShapeDtypeStruct((bsz, s, d), F32),
        scratch_shapes=[
            pltpu.VMEM((tile + SUBLANES, width), F32),
            pltpu.VMEM((tile, width), F32),
            pltpu.VMEM((tile, width), F32),
            pltpu.VMEM((tile, width), F32),
            pltpu.VMEM((SUBLANES, width), F32),
        ],
        compiler_params=pltpu.CompilerParams(
            dimension_semantics=("arbitrary", "arbitrary"), vmem_limit_bytes=VMEM_LIMIT_BYTES),
        name="lru_layer",
    )(x, g, w_in, conv_w, conv_b, wg, bg, a_param, w_out)


def _fox_in_kernel(x_ref, g_ref, wqt_ref, wk_ref, wvt_ref, wgt_ref, wf_ref, wft_ref, bf_ref, bft_ref,
                   sel_ref, ones_ref, ka_ref, qt_ref, vt_ref, sgt_ref, ct_ref, car_ref, cart_ref,
                   *, tile, heads, head_dim, scale):
    t = pl.program_id(1)

    @pl.when(t == 0)
    def _():
        car_ref[...] = jnp.zeros_like(car_ref)
        cart_ref[...] = jnp.zeros_like(cart_ref)

    hn = _rmsnorm(x_ref[...], g_ref[...]).astype(BF16)

    qt_ref[...] = (_dot_nt(wqt_ref[...], hn) * scale).astype(BF16)
    vt_ref[...] = _dot_nt(wvt_ref[...], hn).astype(BF16)
    gt = _dot_nt(wgt_ref[...], hn)
    sgt_ref[...] = (gt * _sigmoid(gt)).astype(BF16)

    row = lax.broadcasted_iota(jnp.int32, (tile, tile), 0)
    col = lax.broadcasted_iota(jnp.int32, (tile, tile), 1)
    tri_lower = (col <= row).astype(BF16)
    tri_upper = (row <= col).astype(BF16)

    lft = _log_sigmoid(_dot_nt(wft_ref[...], hn) + bft_ref[...][:, 0:1])
    hrow = lax.broadcasted_iota(jnp.int32, lft.shape, 0)
    lft = jnp.where(hrow < heads, lft, 0.0)
    p1, p2, p3 = _split3(lft)
    ct = _dot(p1, tri_upper) + _dot(p2, tri_upper) + _dot(p3, tri_upper) + cart_ref[...][:, 0:1]
    cart_ref[...] = jnp.broadcast_to(ct[:, tile - 1:tile], cart_ref.shape)
    ct_ref[...] = ct[0:heads, :]

    lf = _log_sigmoid(_dot(hn, wf_ref[...]) + bf_ref[...])
    hcol = lax.broadcasted_iota(jnp.int32, lf.shape, 1)
    lf = jnp.where(hcol < heads, lf, 0.0)
    p1, p2, p3 = _split3(lf)
    c = _dot(tri_lower, p1) + _dot(tri_lower, p2) + _dot(tri_lower, p3) + car_ref[0:1, :]
    car_ref[0:1, :] = c[tile - 1:tile, :]

    c1, c2, c3 = _split3(c)
    aug = _dot(c1, sel_ref[0]) + _dot(c2, sel_ref[1]) + _dot(c3, sel_ref[2]) + ones_ref[...]

    k = _dot(hn, wk_ref[...])
    lane = lax.broadcasted_iota(jnp.int32, (tile, LANES), 1)
    is_key = lane < head_dim
    for p in range(heads // 2):
        slab = k[:, p * LANES:(p + 1) * LANES]
        ka_ref[2 * p] = jnp.where(is_key, slab, aug).astype(BF16)
        ka_ref[2 * p + 1] = jnp.where(is_key, pltpu.roll(slab, head_dim, 1), aug).astype(BF16)


def _fox_in(x, g, wqt, wk, wvt, wgt, wf, wft, bf, bft, sel, ones, heads, head_dim):
    bsz, s, d = x.shape
    fw = heads * head_dim
    tile = FOX_TILE
    kern = functools.partial(_fox_in_kernel, tile=tile, heads=heads, head_dim=head_dim,
                             scale=1.0 / (head_dim ** 0.5))
    const = lambda *shape: pl.BlockSpec(shape, lambda b, t: (0,) * len(shape))
    feat_major = pl.BlockSpec((None, fw, tile), lambda b, t: (b, 0, t))
    return pl.pallas_call(
        kern,
        grid=(bsz, s // tile),
        in_specs=[
            pl.BlockSpec((None, tile, d), lambda b, t: (b, t, 0)),
            const(1, d),
            const(fw, d), const(d, fw), const(fw, d), const(fw, d),
            const(d, LANES), const(LANES, d), const(1, LANES), const(LANES, LANES),
            const(3, LANES, LANES), const(1, LANES),
        ],
        out_specs=[
            pl.BlockSpec((None, heads, tile, LANES), lambda b, t: (b, 0, t, 0)),
            feat_major, feat_major, feat_major,
            pl.BlockSpec((None, heads, tile), lambda b, t: (b, 0, t)),
        ],
        out_shape=[
            jax.ShapeDtypeStruct((bsz, heads, s, LANES), BF16),
            jax.ShapeDtypeStruct((bsz, fw, s), BF16),
            jax.ShapeDtypeStruct((bsz, fw, s), BF16),
            jax.ShapeDtypeStruct((bsz, fw, s), BF16),
            jax.ShapeDtypeStruct((bsz, heads, s), F32),
        ],
        scratch_shapes=[
            pltpu.VMEM((SUBLANES, LANES), F32),
            pltpu.VMEM((LANES, LANES), F32),
        ],
        compiler_params=pltpu.CompilerParams(
            dimension_semantics=("arbitrary", "arbitrary"), vmem_limit_bytes=VMEM_LIMIT_BYTES),
        name="fox_in",
    )(x, g, wqt, wk, wvt, wgt, wf, wft, bf, bft, sel, ones)


def _attn_kernel(qt_ref, ct_ref, ka_ref, vt_ref, sgt_ref, yt_ref, qa_ref, vx_ref,
                 *, tile, head_dim, n_kv):
    h = pl.program_id(1)
    qi = pl.program_id(2)
    ones_rows = vx_ref.shape[1] - head_dim

    @pl.when(qi == 0)
    def _():
        for j in range(n_kv):
            vx_ref[j, 0:head_dim, :] = vt_ref[:, j * tile:(j + 1) * tile]
            vx_ref[j, head_dim:, :] = jnp.ones((ones_rows, tile), BF16)

    qa_ref[0:head_dim, :] = qt_ref[...]
    c1, c2, c3 = _split3(ct_ref[pl.ds(h, 1), :])
    arow = lax.broadcasted_iota(jnp.int32, (LANES - head_dim, tile), 0)
    aug = jnp.where((arow >= 3 * h) & (arow < 3 * h + 3), 1.0, 0.0)
    cbase = C_LANE - head_dim
    aug = jnp.where(arow == cbase, c1.astype(F32), aug)
    aug = jnp.where(arow == cbase + 1, c2.astype(F32), aug)
    aug = jnp.where(arow == cbase + 2, c3.astype(F32), aug)
    qa_ref[head_dim:, :] = aug.astype(BF16)
    qa = qa_ref[...]

    def block(j, carry, masked):
        m, acc = carry
        start = pl.multiple_of(j * tile, tile)
        s = _dot(ka_ref[pl.ds(start, tile), :], qa)
        if masked:
            kr = lax.broadcasted_iota(jnp.int32, s.shape, 0)
            qc = lax.broadcasted_iota(jnp.int32, s.shape, 1)
            s = jnp.where(kr <= qc, s, NEG_INF)
        m_new = jnp.maximum(m, jnp.max(s, axis=0, keepdims=True))
        alpha = jnp.exp(m - m_new)
        p = jnp.exp(s - m_new).astype(BF16)
        acc = acc * alpha + _dot(vx_ref[j], p)
        return m_new, acc

    m0 = jnp.full((1, tile), NEG_INF, F32)
    acc0 = jnp.zeros((vx_ref.shape[1], tile), F32)
    m, acc = lax.fori_loop(0, qi, lambda j, c: block(j, c, False), (m0, acc0))
    m, acc = block(qi, (m, acc), True)

    o = acc[0:head_dim, :] / acc[head_dim:head_dim + 1, :]
    yt_ref[...] = (o * sgt_ref[...].astype(F32)).astype(BF16)


def _fox_attn(qt, ct, ka, vt, sgt, heads, head_dim):
    bsz, fw, s = qt.shape
    tile = ATTN_TILE
    n_kv = s // tile
    kern = functools.partial(_attn_kernel, tile=tile, head_dim=head_dim, n_kv=n_kv)
    q_block = pl.BlockSpec((None, head_dim, tile), lambda b, h, q: (b, h, q))
    return pl.pallas_call(
        kern,
        grid=(bsz, heads, n_kv),
        in_specs=[
            q_block,
            pl.BlockSpec((None, heads, tile), lambda b, h, q: (b, 0, q)),
            pl.BlockSpec((None, None, s, LANES), lambda b, h, q: (b, h, 0, 0)),
            pl.BlockSpec((None, head_dim, s), lambda b, h, q: (b, h, 0)),
            q_block,
        ],
        out_specs=q_block,
        out_shape=jax.ShapeDtypeStruct((bsz, fw, s), BF16),
        scratch_shapes=[
            pltpu.VMEM((LANES, tile), BF16),
            pltpu.VMEM((n_kv, head_dim + 16, tile), BF16),
        ],
        compiler_params=pltpu.CompilerParams(
            dimension_semantics=("arbitrary", "arbitrary", "arbitrary"),
            vmem_limit_bytes=VMEM_LIMIT_BYTES),
        name="fox_attn",
    )(qt, ct, ka, vt, sgt)


def _fox_out_kernel(yt_ref, x_ref, wo_ref, g_ref, o_ref):
    y = jnp.transpose(yt_ref[...].astype(F32)).astype(BF16)
    z = x_ref[...] + _dot(y, wo_ref[...])
    o_ref[...] = _rmsnorm(z, g_ref[...])


def _fox_out(yt, x, wo, g):
    bsz, s, d = x.shape
    fw = yt.shape[1]
    tile = FOX_TILE
    return pl.pallas_call(
        _fox_out_kernel,
        grid=(bsz, s // tile),
        in_specs=[
            pl.BlockSpec((None, fw, tile), lambda b, t: (b, 0, t)),
            pl.BlockSpec((None, tile, d), lambda b, t: (b, t, 0)),
            pl.BlockSpec((fw, d), lambda b, t: (0, 0)),
            pl.BlockSpec((1, d), lambda b, t: (0, 0)),
        ],
        out_specs=pl.BlockSpec((None, tile, d), lambda b, t: (b, t, 0)),
        out_shape=jax.ShapeDtypeStruct((bsz, s, d), F32),
        compiler_params=pltpu.CompilerParams(
            dimension_semantics=("arbitrary", "arbitrary"), vmem_limit_bytes=VMEM_LIMIT_BYTES),
        name="fox_out",
    )(yt, x, wo, g)


def _key_aug_constants(heads):
    h = jnp.arange(LANES)[:, None]
    lane = jnp.arange(LANES)[None, :]
    sel = jnp.stack([jnp.where((h < heads) & (lane == 64 + 3 * h + k), -1.0, 0.0) for k in range(3)])
    ones = jnp.where((lane >= C_LANE) & (lane < C_LANE + 3), 1.0, 0.0)
    return sel.astype(BF16), ones.astype(F32)


def kernel(x, norm_g, final_g, lru_w_in, lru_conv_w, lru_conv_b, lru_wa, lru_ba, lru_wx, lru_bx,
           lru_a_param, lru_w_out, fox_w_in, fox_b_f, fox_w_out):
    assert norm_g.shape[0] == 2 and lru_w_in.shape[0] == 1 and fox_w_in.shape[0] == 1
    d = x.shape[-1]
    width = lru_w_out.shape[1]
    n_blocks, blk = lru_wa.shape[1], lru_wa.shape[2]
    heads = fox_b_f.shape[1]
    fw = fox_w_out.shape[1]
    head_dim = fw // heads
    assert head_dim == 64 and heads % 2 == 0 and 64 + 3 * heads <= C_LANE

    wg = jnp.concatenate([lru_wa[0], lru_wx[0]], axis=-1).astype(BF16)
    bg = jnp.concatenate([lru_ba[0].reshape(n_blocks, 1, blk), lru_bx[0].reshape(n_blocks, 1, blk)], axis=-1)
    x1 = _lru_layer(x, norm_g[0:1], lru_w_in[0].astype(BF16), lru_conv_w[0], lru_conv_b[0:1], wg, bg,
                    lru_a_param[0:1], lru_w_out[0].astype(BF16), n_blocks)

    w = fox_w_in[0]
    wqt = w[:, 0 * fw:1 * fw].T.astype(BF16)
    wk = w[:, 1 * fw:2 * fw].astype(BF16)
    wvt = w[:, 2 * fw:3 * fw].T.astype(BF16)
    wgt = w[:, 3 * fw:4 * fw].T.astype(BF16)
    wf = jnp.pad(w[:, 4 * fw:], ((0, 0), (0, LANES - heads))).astype(BF16)
    bf = jnp.pad(fox_b_f[0:1], ((0, 0), (0, LANES - heads)))
    bft = jnp.broadcast_to(bf.T, (LANES, LANES))
    sel, ones = _key_aug_constants(heads)
    ka, qt, vt, sgt, ct = _fox_in(x1, norm_g[1:2], wqt, wk, wvt, wgt, wf, wf.T, bf, bft, sel, ones,
                                  heads, head_dim)
    yt = _fox_attn(qt, ct, ka, vt, sgt, heads, head_dim)
    return _fox_out(yt, x1, fox_w_out[0].astype(BF16), final_g.reshape(1, d))
```

```python
import functools

import jax
import jax.numpy as jnp
from jax import lax
from jax.experimental import pallas as pl
from jax.experimental.pallas import tpu as pltpu

F32 = jnp.float32
BF16 = jnp.bfloat16

EPS = 1e-6
LRU_C = 8.0
CONV_WIDTH = 4
NEG_INF = -1e30
LOG2E = 1.4426950408889634

LANES = 128
SUBLANES = 8
VMEM_LIMIT_BYTES = 56 * 1024 * 1024

LRU_TILE = 256
FOX_TILE = 512
ATTN_TQ = 1024
ATTN_TK = 256

C_LANE = 112


def _rmsnorm(x, g):
    return x * lax.rsqrt(jnp.mean(x * x, axis=-1, keepdims=True) + EPS) * g


def _sigmoid(x):
    return 1.0 / (1.0 + jnp.exp(-x))


def _log_sigmoid(x):
    return jnp.minimum(x, 0.0) - jnp.log1p(jnp.exp(-jnp.abs(x)))


def _split3(c):
    c1 = c.astype(BF16)
    r1 = c - c1.astype(F32)
    c2 = r1.astype(BF16)
    c3 = (r1 - c2.astype(F32)).astype(BF16)
    return c1, c2, c3


def _dot(a, b):
    return jnp.dot(a, b, preferred_element_type=F32)


def _dot_nt(a, b):
    return lax.dot_general(a, b, (((1,), (1,)), ((), ())), preferred_element_type=F32)


def _lru_kernel(x_ref, g_ref, win_ref, cw_ref, cb_ref, wg_ref, bg_ref, ap_ref, wout_ref, o_ref,
                ext_ref, a_ref, b_ref, hs_ref, hcar_ref, *, tile, width, n_blocks):
    t = pl.program_id(1)
    blk = width // n_blocks

    @pl.when(t == 0)
    def _():
        ext_ref[0:SUBLANES, :] = jnp.zeros((SUBLANES, width), F32)
        hcar_ref[...] = jnp.zeros_like(hcar_ref)

    x = x_ref[...]
    xn = _rmsnorm(x, g_ref[...]).astype(BF16)
    u = _dot(xn, win_ref[...])
    gate = u[:, width:]
    ext_ref[SUBLANES:SUBLANES + tile, :] = u[:, :width]

    cw = cw_ref[...]
    xc = cb_ref[...] + cw[3:4] * ext_ref[SUBLANES:SUBLANES + tile, :]
    for k in range(CONV_WIDTH - 1):
        off = SUBLANES - (CONV_WIDTH - 1) + k
        xc = xc + cw[k:k + 1] * ext_ref[off:off + tile, :]
    ext_ref[0:SUBLANES, :] = ext_ref[tile:tile + SUBLANES, :]

    ap = ap_ref[...]
    sp = jnp.maximum(-ap, 0.0) + jnp.log1p(jnp.exp(-jnp.abs(ap)))
    xcb = xc.astype(BF16)
    for n in range(n_blocks):
        sl = slice(n * blk, (n + 1) * blk)
        g = _dot(xcb[:, sl], wg_ref[n]) + bg_ref[n]
        r = _sigmoid(g[:, :blk])
        i = _sigmoid(g[:, blk:])
        neg_log_a = LRU_C * r * sp[:, sl]
        a = jnp.exp(-neg_log_a)
        a_ref[:, sl] = a
        b_ref[:, sl] = jnp.sqrt(jnp.tanh(neg_log_a) * (1.0 + a * a)) * (i * xc[:, sl])

    def step(i, h):
        h = a_ref[pl.ds(i, 1), :] * h + b_ref[pl.ds(i, 1), :]
        hs_ref[pl.ds(i, 1), :] = h
        return h

    h = lax.fori_loop(0, tile, step, hcar_ref[0:1, :], unroll=SUBLANES)
    hcar_ref[0:1, :] = h

    y = hs_ref[...] * (gate * _sigmoid(gate))
    o_ref[...] = x + _dot(y.astype(BF16), wout_ref[...])


def _lru_layer(x, g, w_in, conv_w, conv_b, wg, bg, a_param, w_out, n_blocks):
    bsz, s, d = x.shape
    width = w_out.shape[0]
    tile = LRU_TILE
    kern = functools.partial(_lru_kernel, tile=tile, width=width, n_blocks=n_blocks)
    const = lambda *shape: pl.BlockSpec(shape, lambda b, t: (0,) * len(shape))
    return pl.pallas_call(
        kern,
        grid=(bsz, s // tile),
        in_specs=[
            pl.BlockSpec((None, tile, d), lambda b, t: (b, t, 0)),
            const(1, d),
            const(d, 2 * width),
            const(CONV_WIDTH, width),
            const(1, width),
            const(n_blocks, width // n_blocks, 2 * (width // n_blocks)),
            const(n_blocks, 1, 2 * (width // n_blocks)),
            const(1, width),
            const(width, d),
        ],
        out_specs=pl.BlockSpec((None, tile, d), lambda b, t: (b, t, 0)),
        out_shape=jax.ShapeDtypeStruct((bsz, s, d), F32),
        scratch_shapes=[
            pltpu.VMEM((tile + SUBLANES, width), F32),
            pltpu.VMEM((tile, width), F32),
            pltpu.VMEM((tile, width), F32),
            pltpu.VMEM((tile, width), F32),
            pltpu.VMEM((SUBLANES, width), F32),
        ],
        compiler_params=pltpu.CompilerParams(
            dimension_semantics=("arbitrary", "arbitrary"), vmem_limit_bytes=VMEM_LIMIT_BYTES),
        name="lru_layer",
    )(x, g, w_in, conv_w, conv_b, wg, bg, a_param, w_out)


def _fox_in_kernel(x_ref, g_ref, wqt_ref, wk_ref, wvt_ref, wgt_ref, wf_ref, wft_ref, bf_ref, bft_ref,
                   sel_ref, ones_ref, ka_ref, qt_ref, vt_ref, sgt_ref, ct_ref, car_ref, cart_ref,
                   *, tile, heads, head_dim, scale):
    t = pl.program_id(1)

    @pl.when(t == 0)
    def _():
        car_ref[...] = jnp.zeros_like(car_ref)
        cart_ref[...] = jnp.zeros_like(cart_ref)

    hn = _rmsnorm(x_ref[...], g_ref[...]).astype(BF16)

    qt_ref[...] = (_dot_nt(wqt_ref[...], hn) * scale).astype(BF16)
    vt_ref[...] = _dot_nt(wvt_ref[...], hn).astype(BF16)
    gt = _dot_nt(wgt_ref[...], hn)
    sgt_ref[...] = (gt * _sigmoid(gt)).astype(BF16)

    row = lax.broadcasted_iota(jnp.int32, (tile, tile), 0)
    col = lax.broadcasted_iota(jnp.int32, (tile, tile), 1)
    tri_lower = (col <= row).astype(BF16)
    tri_upper = (row <= col).astype(BF16)

    lft = _log_sigmoid(_dot_nt(wft_ref[...], hn) + bft_ref[...][:, 0:1])
    hrow = lax.broadcasted_iota(jnp.int32, lft.shape, 0)
    lft = jnp.where(hrow < heads, lft, 0.0)
    p1, p2, p3 = _split3(lft)
    ct = _dot(p1, tri_upper) + _dot(p2, tri_upper) + _dot(p3, tri_upper) + cart_ref[...][:, 0:1]
    cart_ref[...] = jnp.broadcast_to(ct[:, tile - 1:tile], cart_ref.shape)
    ct_ref[...] = ct[0:heads, :]

    lf = _log_sigmoid(_dot(hn, wf_ref[...]) + bf_ref[...])
    hcol = lax.broadcasted_iota(jnp.int32, lf.shape, 1)
    lf = jnp.where(hcol < heads, lf, 0.0)
    p1, p2, p3 = _split3(lf)
    c = _dot(tri_lower, p1) + _dot(tri_lower, p2) + _dot(tri_lower, p3) + car_ref[0:1, :]
    car_ref[0:1, :] = c[tile - 1:tile, :]

    c1, c2, c3 = _split3(c * LOG2E)
    aug = _dot(c1, sel_ref[0]) + _dot(c2, sel_ref[1]) + _dot(c3, sel_ref[2]) + ones_ref[...]

    k = _dot(hn, wk_ref[...])
    lane = lax.broadcasted_iota(jnp.int32, (tile, LANES), 1)
    is_key = lane < head_dim
    for p in range(heads // 2):
        slab = k[:, p * LANES:(p + 1) * LANES]
        ka_ref[2 * p] = jnp.where(is_key, slab, aug).astype(BF16)
        ka_ref[2 * p + 1] = jnp.where(is_key, pltpu.roll(slab, head_dim, 1), aug).astype(BF16)


def _fox_in(x, g, wqt, wk, wvt, wgt, wf, wft, bf, bft, sel, ones, heads, head_dim):
    bsz, s, d = x.shape
    fw = heads * head_dim
    tile = FOX_TILE
    kern = functools.partial(_fox_in_kernel, tile=tile, heads=heads, head_dim=head_dim,
                             scale=LOG2E / (head_dim ** 0.5))
    const = lambda *shape: pl.BlockSpec(shape, lambda b, t: (0,) * len(shape))
    feat_major = pl.BlockSpec((None, fw, tile), lambda b, t: (b, 0, t))
    return pl.pallas_call(
        kern,
        grid=(bsz, s // tile),
        in_specs=[
            pl.BlockSpec((None, tile, d), lambda b, t: (b, t, 0)),
            const(1, d),
            const(fw, d), const(d, fw), const(fw, d), const(fw, d),
            const(d, LANES), const(LANES, d), const(1, LANES), const(LANES, LANES),
            const(3, LANES, LANES), const(1, LANES),
        ],
        out_specs=[
            pl.BlockSpec((None, heads, tile, LANES), lambda b, t: (b, 0, t, 0)),
            feat_major, feat_major, feat_major,
            pl.BlockSpec((None, heads, tile), lambda b, t: (b, 0, t)),
        ],
        out_shape=[
            jax.ShapeDtypeStruct((bsz, heads, s, LANES), BF16),
            jax.ShapeDtypeStruct((bsz, fw, s), BF16),
            jax.ShapeDtypeStruct((bsz, fw, s), BF16),
            jax.ShapeDtypeStruct((bsz, fw, s), BF16),
            jax.ShapeDtypeStruct((bsz, heads, s), F32),
        ],
        scratch_shapes=[
            pltpu.VMEM((SUBLANES, LANES), F32),
            pltpu.VMEM((LANES, LANES), F32),
        ],
        compiler_params=pltpu.CompilerParams(
            dimension_semantics=("arbitrary", "arbitrary"), vmem_limit_bytes=VMEM_LIMIT_BYTES),
        name="fox_in",
    )(x, g, wqt, wk, wvt, wgt, wf, wft, bf, bft, sel, ones)


def _attn_kernel(qt_ref, ct_ref, ka_ref, vt_ref, sgt_ref, yt_ref, qa_ref, vx_ref, s0_ref, s1_ref,
                 *, tq, tk, head_dim, n_kv):
    h = pl.program_id(1)
    qi = pl.program_id(2)
    ones_rows = vx_ref.shape[1] - head_dim
    n_sub = tq // tk
    assert n_sub * tk == tq and n_sub % 2 == 0

    @pl.when(qi == 0)
    def _():
        for j in range(n_kv):
            vx_ref[j, 0:head_dim, :] = vt_ref[:, j * tk:(j + 1) * tk]
            vx_ref[j, head_dim:, :] = jnp.ones((ones_rows, tk), BF16)

    qa_ref[0:head_dim, :] = qt_ref[...]
    c1, c2, c3 = _split3(ct_ref[pl.ds(h, 1), :] * LOG2E)
    arow = lax.broadcasted_iota(jnp.int32, (LANES - head_dim, tq), 0)
    aug = jnp.where((arow >= 3 * h) & (arow < 3 * h + 3), 1.0, 0.0)
    cbase = C_LANE - head_dim
    aug = jnp.where(arow == cbase, c1.astype(F32), aug)
    aug = jnp.where(arow == cbase + 1, c2.astype(F32), aug)
    aug = jnp.where(arow == cbase + 2, c3.astype(F32), aug)
    qa_ref[head_dim:, :] = aug.astype(BF16)

    def scores(j, qlo=0):
        start = pl.multiple_of(j * tk, tk)
        return _dot(ka_ref[pl.ds(start, tk), :], qa_ref[:, qlo:])

    def update(s, j, carry, qlo=None):
        m, acc = carry
        if qlo is not None:
            m_all, acc_all = carry
            m, acc = m_all[:, qlo:], acc_all[:, qlo:]
            kr = lax.broadcasted_iota(jnp.int32, s.shape, 0)
            qc = lax.broadcasted_iota(jnp.int32, s.shape, 1)
            s = jnp.where(kr <= qc, s, NEG_INF)
        m_new = jnp.maximum(m, jnp.max(s, axis=0, keepdims=True))
        alpha = jnp.exp2(m - m_new)
        p = jnp.exp2(s - m_new).astype(BF16)
        acc = acc * alpha + _dot(vx_ref[j], p)
        if qlo:
            m_new = jnp.concatenate([m_all[:, :qlo], m_new], axis=1)
            acc = jnp.concatenate([acc_all[:, :qlo], acc], axis=1)
        return m_new, acc

    s_refs = (s0_ref, s1_ref)

    def group(i, carry, diagonal):
        for u in range(n_sub):
            j = n_sub * i + u
            if not (diagonal and u == n_sub - 1):
                qlo_next = (u + 1) * tk if diagonal else 0
                s_refs[(u + 1) % 2][:, 0:tq - qlo_next] = scores(j + 1, qlo_next)
            qlo = u * tk if diagonal else 0
            carry = update(s_refs[u % 2][:, 0:tq - qlo], j, carry, qlo=qlo if diagonal else None)
        return carry

    m0 = jnp.full((1, tq), NEG_INF, F32)
    acc0 = jnp.zeros((vx_ref.shape[1], tq), F32)
    s0_ref[...] = scores(0)
    carry = lax.fori_loop(0, qi, lambda i, c: group(i, c, False), (m0, acc0))
    m, acc = group(qi, carry, True)

    o = acc[0:head_dim, :] / acc[head_dim:head_dim + 1, :]
    yt_ref[...] = (o * sgt_ref[...].astype(F32)).astype(BF16)


def _fox_attn(qt, ct, ka, vt, sgt, heads, head_dim):
    bsz, fw, s = qt.shape
    tq, tk = ATTN_TQ, ATTN_TK
    n_kv = s // tk
    kern = functools.partial(_attn_kernel, tq=tq, tk=tk, head_dim=head_dim, n_kv=n_kv)
    q_block = pl.BlockSpec((None, head_dim, tq), lambda b, h, q: (b, h, q))
    return pl.pallas_call(
        kern,
        grid=(bsz, heads, s // tq),
        in_specs=[
            q_block,
            pl.BlockSpec((None, heads, tq), lambda b, h, q: (b, 0, q)),
            pl.BlockSpec((None, None, s, LANES), lambda b, h, q: (b, h, 0, 0)),
            pl.BlockSpec((None, head_dim, s), lambda b, h, q: (b, h, 0)),
            q_block,
        ],
        out_specs=q_block,
        out_shape=jax.ShapeDtypeStruct((bsz, fw, s), BF16),
        scratch_shapes=[
            pltpu.VMEM((LANES, tq), BF16),
            pltpu.VMEM((n_kv, head_dim + 16, tk), BF16),
            pltpu.VMEM((tk, tq), F32),
            pltpu.VMEM((tk, tq), F32),
        ],
        compiler_params=pltpu.CompilerParams(
            dimension_semantics=("arbitrary", "arbitrary", "arbitrary"),
            vmem_limit_bytes=VMEM_LIMIT_BYTES),
        name="fox_attn",
    )(qt, ct, ka, vt, sgt)


def _fox_out_kernel(yt_ref, x_ref, wo_ref, g_ref, o_ref):
    y = jnp.transpose(yt_ref[...].astype(F32)).astype(BF16)
    z = x_ref[...] + _dot(y, wo_ref[...])
    o_ref[...] = _rmsnorm(z, g_ref[...])


def _fox_out(yt, x, wo, g):
    bsz, s, d = x.shape
    fw = yt.shape[1]
    tile = FOX_TILE
    return pl.pallas_call(
        _fox_out_kernel,
        grid=(bsz, s // tile),
        in_specs=[
            pl.BlockSpec((None, fw, tile), lambda b, t: (b, 0, t)),
            pl.BlockSpec((None, tile, d), lambda b, t: (b, t, 0)),
            pl.BlockSpec((fw, d), lambda b, t: (0, 0)),
            pl.BlockSpec((1, d), lambda b, t: (0, 0)),
        ],
        out_specs=pl.BlockSpec((None, tile, d), lambda b, t: (b, t, 0)),
        out_shape=jax.ShapeDtypeStruct((bsz, s, d), F32),
        compiler_params=pltpu.CompilerParams(
            dimension_semantics=("arbitrary", "arbitrary"), vmem_limit_bytes=VMEM_LIMIT_BYTES),
        name="fox_out",
    )(yt, x, wo, g)


def _key_aug_constants(heads):
    h = jnp.arange(LANES)[:, None]
    lane = jnp.arange(LANES)[None, :]
    sel = jnp.stack([jnp.where((h < heads) & (lane == 64 + 3 * h + k), -1.0, 0.0) for k in range(3)])
    ones = jnp.where((lane >= C_LANE) & (lane < C_LANE + 3), 1.0, 0.0)
    return sel.astype(BF16), ones.astype(F32)


def kernel(x, norm_g, final_g, lru_w_in, lru_conv_w, lru_conv_b, lru_wa, lru_ba, lru_wx, lru_bx,
           lru_a_param, lru_w_out, fox_w_in, fox_b_f, fox_w_out):
    assert norm_g.shape[0] == 2 and lru_w_in.shape[0] == 1 and fox_w_in.shape[0] == 1
    d = x.shape[-1]
    width = lru_w_out.shape[1]
    n_blocks, blk = lru_wa.shape[1], lru_wa.shape[2]
    heads = fox_b_f.shape[1]
    fw = fox_w_out.shape[1]
    head_dim = fw // heads
    assert head_dim == 64 and heads % 2 == 0 and 64 + 3 * heads <= C_LANE

    wg = jnp.concatenate([lru_wa[0], lru_wx[0]], axis=-1).astype(BF16)
    bg = jnp.concatenate([lru_ba[0].reshape(n_blocks, 1, blk), lru_bx[0].reshape(n_blocks, 1, blk)], axis=-1)
    x1 = _lru_layer(x, norm_g[0:1], lru_w_in[0].astype(BF16), lru_conv_w[0], lru_conv_b[0:1], wg, bg,
                    lru_a_param[0:1], lru_w_out[0].astype(BF16), n_blocks)

    w = fox_w_in[0]
    wqt = w[:, 0 * fw:1 * fw].T.astype(BF16)
    wk = w[:, 1 * fw:2 * fw].astype(BF16)
    wvt = w[:, 2 * fw:3 * fw].T.astype(BF16)
    wgt = w[:, 3 * fw:4 * fw].T.astype(BF16)
    wf = jnp.pad(w[:, 4 * fw:], ((0, 0), (0, LANES - heads))).astype(BF16)
    bf = jnp.pad(fox_b_f[0:1], ((0, 0), (0, LANES - heads)))
    bft = jnp.broadcast_to(bf.T, (LANES, LANES))
    sel, ones = _key_aug_constants(heads)
    ka, qt, vt, sgt, ct = _fox_in(x1, norm_g[1:2], wqt, wk, wvt, wgt, wf, wf.T, bf, bft, sel, ones,
                                  heads, head_dim)
    yt = _fox_attn(qt, ct, ka, vt, sgt, heads, head_dim)
    return _fox_out(yt, x1, fox_w_out[0].astype(BF16), final_g.reshape(1, d))
```

```python
import functools

import jax
import jax.numpy as jnp
from jax import lax
from jax.experimental import pallas as pl
from jax.experimental.pallas import tpu as pltpu

F32 = jnp.float32
BF16 = jnp.bfloat16

EPS = 1e-6
LRU_C = 8.0
CONV_WIDTH = 4
NEG_INF = -1e30
LOG2E = 1.4426950408889634

LANES = 128
SUBLANES = 8
VMEM_LIMIT_BYTES = 56 * 1024 * 1024

LRU_TILE = 256
FOX_TILE = 512
ATTN_TQ = 2048
ATTN_TK = 256

C_LANE = 112


def _rmsnorm(x, g):
    return x * lax.rsqrt(jnp.mean(x * x, axis=-1, keepdims=True) + EPS) * g


def _sigmoid(x):
    return 0.5 + 0.5 * jnp.tanh(0.5 * x)


def _log_sigmoid(x):
    return jnp.minimum(x, 0.0) - jnp.log1p(jnp.exp(-jnp.abs(x)))


def _split3(c):
    c1 = c.astype(BF16)
    r1 = c - c1.astype(F32)
    c2 = r1.astype(BF16)
    c3 = (r1 - c2.astype(F32)).astype(BF16)
    return c1, c2, c3


def _dot(a, b):
    return jnp.dot(a, b, preferred_element_type=F32)


def _dot_nt(a, b):
    return lax.dot_general(a, b, (((1,), (1,)), ((), ())), preferred_element_type=F32)


def _lru_kernel(x_ref, g_ref, win_ref, cw_ref, cb_ref, wg_ref, bg_ref, ap_ref, wout_ref, o_ref,
                ext_ref, a_ref, b_ref, hs_ref, hcar_ref, *, tile, width, n_blocks):
    t = pl.program_id(1)
    blk = width // n_blocks

    @pl.when(t == 0)
    def _():
        ext_ref[0:SUBLANES, :] = jnp.zeros((SUBLANES, width), F32)
        hcar_ref[...] = jnp.zeros_like(hcar_ref)

    x = x_ref[...]
    xn = _rmsnorm(x, g_ref[...]).astype(BF16)
    u = _dot(xn, win_ref[...])
    gate = u[:, width:]
    ext_ref[SUBLANES:SUBLANES + tile, :] = u[:, :width]

    cw = cw_ref[...]
    xc = cb_ref[...] + cw[3:4] * ext_ref[SUBLANES:SUBLANES + tile, :]
    for k in range(CONV_WIDTH - 1):
        off = SUBLANES - (CONV_WIDTH - 1) + k
        xc = xc + cw[k:k + 1] * ext_ref[off:off + tile, :]
    ext_ref[0:SUBLANES, :] = ext_ref[tile:tile + SUBLANES, :]

    ap = ap_ref[...]
    sp = jnp.maximum(-ap, 0.0) + jnp.log1p(jnp.exp(-jnp.abs(ap)))
    xcb = xc.astype(BF16)
    for n in range(n_blocks):
        sl = slice(n * blk, (n + 1) * blk)
        g = _dot(xcb[:, sl], wg_ref[n]) + bg_ref[n]
        r = _sigmoid(g[:, :blk])
        i = _sigmoid(g[:, blk:])
        neg_log_a = LRU_C * r * sp[:, sl]
        a = jnp.exp(-neg_log_a)
        a_ref[:, sl] = a
        one_minus_a2 = jnp.tanh(neg_log_a) * (1.0 + a * a)
        mult = jnp.where(one_minus_a2 > 0.0, one_minus_a2 * lax.rsqrt(one_minus_a2), 0.0)
        b_ref[:, sl] = mult * (i * xc[:, sl])

    def step(i, h):
        h = a_ref[pl.ds(i, 1), :] * h + b_ref[pl.ds(i, 1), :]
        hs_ref[pl.ds(i, 1), :] = h
        return h

    h = lax.fori_loop(0, tile, step, hcar_ref[0:1, :], unroll=SUBLANES)
    hcar_ref[0:1, :] = h

    y = hs_ref[...] * (gate * _sigmoid(gate))
    o_ref[...] = x + _dot(y.astype(BF16), wout_ref[...])


def _lru_layer(x, g, w_in, conv_w, conv_b, wg, bg, a_param, w_out, n_blocks):
    bsz, s, d = x.shape
    width = w_out.shape[0]
    tile = LRU_TILE
    kern = functools.partial(_lru_kernel, tile=tile, width=width, n_blocks=n_blocks)
    const = lambda *shape: pl.BlockSpec(shape, lambda b, t: (0,) * len(shape))
    return pl.pallas_call(
        kern,
        grid=(bsz, s // tile),
        in_specs=[
            pl.BlockSpec((None, tile, d), lambda b, t: (b, t, 0)),
            const(1, d),
            const(d, 2 * width),
            const(CONV_WIDTH, width),
            const(1, width),
            const(n_blocks, width // n_blocks, 2 * (width // n_blocks)),
            const(n_blocks, 1, 2 * (width // n_blocks)),
            const(1, width),
            const(width, d),
        ],
        out_specs=pl.BlockSpec((None, tile, d), lambda b, t: (b, t, 0)),
        out_shape=jax.ShapeDtypeStruct((bsz, s, d), F32),
        scratch_shapes=[
            pltpu.VMEM((tile + SUBLANES, width), F32),
            pltpu.VMEM((tile, width), F32),
            pltpu.VMEM((tile, width), F32),
            pltpu.VMEM((tile, width), F32),
            pltpu.VMEM((SUBLANES, width), F32),
        ],
        compiler_params=pltpu.CompilerParams(
            dimension_semantics=("arbitrary", "arbitrary"), vmem_limit_bytes=VMEM_LIMIT_BYTES),
        name="lru_layer",
    )(x, g, w_in, conv_w, conv_b, wg, bg, a_param, w_out)


def _fox_in_kernel(x_ref, g_ref, wqt_ref, wk_ref, wvt_ref, wgt_ref, wft_ref, bft_ref,
                   sel_ref, ones_ref, ka_ref, qt_ref, vt_ref, sgt_ref, ct_ref, cart_ref,
                   *, tile, heads, head_dim, scale):
    t = pl.program_id(1)

    @pl.when(t == 0)
    def _():
        cart_ref[...] = jnp.zeros_like(cart_ref)

    hn = _rmsnorm(x_ref[...], g_ref[...]).astype(BF16)

    qt_ref[...] = (_dot_nt(wqt_ref[...], hn) * scale).astype(BF16)
    vt_ref[...] = _dot_nt(wvt_ref[...], hn).astype(BF16)
    gt = _dot_nt(wgt_ref[...], hn)
    sgt_ref[...] = (gt * _sigmoid(gt)).astype(BF16)

    row = lax.broadcasted_iota(jnp.int32, (tile, tile), 0)
    col = lax.broadcasted_iota(jnp.int32, (tile, tile), 1)
    tri_upper = (row <= col).astype(BF16)

    lft = _log_sigmoid(_dot_nt(wft_ref[...], hn) + bft_ref[...][:, 0:1])
    hrow = lax.broadcasted_iota(jnp.int32, lft.shape, 0)
    lft = jnp.where(hrow < heads, lft, 0.0)
    p1, p2, p3 = _split3(lft)
    ct = _dot(p1, tri_upper) + _dot(p2, tri_upper) + _dot(p3, tri_upper) + cart_ref[...][:, 0:1]
    cart_ref[...] = jnp.broadcast_to(ct[:, tile - 1:tile], cart_ref.shape)
    ct_ref[...] = ct[0:heads, :]

    c1, c2, c3 = _split3(jnp.transpose(ct) * LOG2E)
    aug = _dot(c1, sel_ref[0]) + _dot(c2, sel_ref[1]) + _dot(c3, sel_ref[2]) + ones_ref[...]

    k = _dot(hn, wk_ref[...])
    lane = lax.broadcasted_iota(jnp.int32, (tile, LANES), 1)
    is_key = lane < head_dim
    for p in range(heads // 2):
        slab = k[:, p * LANES:(p + 1) * LANES]
        ka_ref[2 * p] = jnp.where(is_key, slab, aug).astype(BF16)
        ka_ref[2 * p + 1] = jnp.where(is_key, pltpu.roll(slab, head_dim, 1), aug).astype(BF16)


def _fox_in(x, g, wqt, wk, wvt, wgt, wft, bft, sel, ones, heads, head_dim):
    bsz, s, d = x.shape
    fw = heads * head_dim
    tile = FOX_TILE
    kern = functools.partial(_fox_in_kernel, tile=tile, heads=heads, head_dim=head_dim,
                             scale=LOG2E / (head_dim ** 0.5))
    const = lambda *shape: pl.BlockSpec(shape, lambda b, t: (0,) * len(shape))
    feat_major = pl.BlockSpec((None, fw, tile), lambda b, t: (b, 0, t))
    return pl.pallas_call(
        kern,
        grid=(bsz, s // tile),
        in_specs=[
            pl.BlockSpec((None, tile, d), lambda b, t: (b, t, 0)),
            const(1, d),
            const(fw, d), const(d, fw), const(fw, d), const(fw, d),
            const(LANES, d), const(LANES, LANES),
            const(3, LANES, LANES), const(1, LANES),
        ],
        out_specs=[
            pl.BlockSpec((None, heads, tile, LANES), lambda b, t: (b, 0, t, 0)),
            feat_major, feat_major, feat_major,
            pl.BlockSpec((None, heads, tile), lambda b, t: (b, 0, t)),
        ],
        out_shape=[
            jax.ShapeDtypeStruct((bsz, heads, s, LANES), BF16),
            jax.ShapeDtypeStruct((bsz, fw, s), BF16),
            jax.ShapeDtypeStruct((bsz, fw, s), BF16),
            jax.ShapeDtypeStruct((bsz, fw, s), BF16),
            jax.ShapeDtypeStruct((bsz, heads, s), F32),
        ],
        scratch_shapes=[
            pltpu.VMEM((LANES, LANES), F32),
        ],
        compiler_params=pltpu.CompilerParams(
            dimension_semantics=("arbitrary", "arbitrary"), vmem_limit_bytes=VMEM_LIMIT_BYTES),
        name="fox_in",
    )(x, g, wqt, wk, wvt, wgt, wft, bft, sel, ones)


def _attn_kernel(qt_ref, ct_ref, ka_ref, vt_ref, sgt_ref, yt_ref, qa_ref, vx_ref, s0_ref, s1_ref,
                 *, tq, tk, head_dim, n_kv):
    h = pl.program_id(1)
    qi = pl.program_id(2)
    ones_rows = vx_ref.shape[1] - head_dim
    n_sub = tq // tk
    assert n_sub * tk == tq and n_sub % 2 == 0

    @pl.when(qi == 0)
    def _():
        for j in range(n_kv):
            vx_ref[j, 0:head_dim, :] = vt_ref[:, j * tk:(j + 1) * tk]
            vx_ref[j, head_dim:, :] = jnp.ones((ones_rows, tk), BF16)

    qa_ref[0:head_dim, :] = qt_ref[...]
    c1, c2, c3 = _split3(ct_ref[pl.ds(h, 1), :] * LOG2E)
    arow = lax.broadcasted_iota(jnp.int32, (LANES - head_dim, tq), 0)
    aug = jnp.where((arow >= 3 * h) & (arow < 3 * h + 3), 1.0, 0.0)
    cbase = C_LANE - head_dim
    aug = jnp.where(arow == cbase, c1.astype(F32), aug)
    aug = jnp.where(arow == cbase + 1, c2.astype(F32), aug)
    aug = jnp.where(arow == cbase + 2, c3.astype(F32), aug)
    qa_ref[head_dim:, :] = aug.astype(BF16)

    def scores(j, qlo=0):
        start = pl.multiple_of(j * tk, tk)
        return _dot(ka_ref[pl.ds(start, tk), :], qa_ref[:, qlo:])

    def update(s, j, carry, qlo=None):
        m, acc = carry
        if qlo is not None:
            m_all, acc_all = carry
            m, acc = m_all[:, qlo:], acc_all[:, qlo:]
            kr = lax.broadcasted_iota(jnp.int32, s.shape, 0)
            qc = lax.broadcasted_iota(jnp.int32, s.shape, 1)
            s = jnp.where(kr <= qc, s, NEG_INF)
        m_new = jnp.maximum(m, jnp.max(s, axis=0, keepdims=True))
        alpha = jnp.exp2(m - m_new)
        p = jnp.exp2(s - m_new).astype(BF16)
        acc = acc * alpha + _dot(vx_ref[j], p)
        if qlo:
            m_new = jnp.concatenate([m_all[:, :qlo], m_new], axis=1)
            acc = jnp.concatenate([acc_all[:, :qlo], acc], axis=1)
        return m_new, acc

    s_refs = (s0_ref, s1_ref)

    def group(i, carry, diagonal):
        for u in range(n_sub):
            j = n_sub * i + u
            if not (diagonal and u == n_sub - 1):
                qlo_next = (u + 1) * tk if diagonal else 0
                s_refs[(u + 1) % 2][:, 0:tq - qlo_next] = scores(j + 1, qlo_next)
            qlo = u * tk if diagonal else 0
            carry = update(s_refs[u % 2][:, 0:tq - qlo], j, carry, qlo=qlo if diagonal else None)
        return carry

    m0 = jnp.full((1, tq), NEG_INF, F32)
    acc0 = jnp.zeros((vx_ref.shape[1], tq), F32)
    s0_ref[...] = scores(0)
    carry = lax.fori_loop(0, qi, lambda i, c: group(i, c, False), (m0, acc0))
    m, acc = group(qi, carry, True)

    o = acc[0:head_dim, :] / acc[head_dim:head_dim + 1, :]
    yt_ref[...] = (o * sgt_ref[...].astype(F32)).astype(BF16)


def _fox_attn(qt, ct, ka, vt, sgt, heads, head_dim):
    bsz, fw, s = qt.shape
    tq, tk = ATTN_TQ, ATTN_TK
    n_kv = s // tk
    kern = functools.partial(_attn_kernel, tq=tq, tk=tk, head_dim=head_dim, n_kv=n_kv)
    q_block = pl.BlockSpec((None, head_dim, tq), lambda b, h, q: (b, h, q))
    return pl.pallas_call(
        kern,
        grid=(bsz, heads, s // tq),
        in_specs=[
            q_block,
            pl.BlockSpec((None, heads, tq), lambda b, h, q: (b, 0, q)),
            pl.BlockSpec((None, None, s, LANES), lambda b, h, q: (b, h, 0, 0)),
            pl.BlockSpec((None, head_dim, s), lambda b, h, q: (b, h, 0)),
            q_block,
        ],
        out_specs=q_block,
        out_shape=jax.ShapeDtypeStruct((bsz, fw, s), BF16),
        scratch_shapes=[
            pltpu.VMEM((LANES, tq), BF16),
            pltpu.VMEM((n_kv, head_dim + 16, tk), BF16),
            pltpu.VMEM((tk, tq), F32),
            pltpu.VMEM((tk, tq), F32),
        ],
        compiler_params=pltpu.CompilerParams(
            dimension_semantics=("arbitrary", "arbitrary", "arbitrary"),
            vmem_limit_bytes=VMEM_LIMIT_BYTES),
        name="fox_attn",
    )(qt, ct, ka, vt, sgt)


def _fox_out_kernel(yt_ref, x_ref, wo_ref, g_ref, o_ref):
    y = jnp.transpose(yt_ref[...].astype(F32)).astype(BF16)
    z = x_ref[...] + _dot(y, wo_ref[...])
    o_ref[...] = _rmsnorm(z, g_ref[...])


def _fox_out(yt, x, wo, g):
    bsz, s, d = x.shape
    fw = yt.shape[1]
    tile = FOX_TILE
    return pl.pallas_call(
        _fox_out_kernel,
        grid=(bsz, s // tile),
        in_specs=[
            pl.BlockSpec((None, fw, tile), lambda b, t: (b, 0, t)),
            pl.BlockSpec((None, tile, d), lambda b, t: (b, t, 0)),
            pl.BlockSpec((fw, d), lambda b, t: (0, 0)),
            pl.BlockSpec((1, d), lambda b, t: (0, 0)),
        ],
        out_specs=pl.BlockSpec((None, tile, d), lambda b, t: (b, t, 0)),
        out_shape=jax.ShapeDtypeStruct((bsz, s, d), F32),
        compiler_params=pltpu.CompilerParams(
            dimension_semantics=("arbitrary", "arbitrary"), vmem_limit_bytes=VMEM_LIMIT_BYTES),
        name="fox_out",
    )(yt, x, wo, g)


def _key_aug_constants(heads):
    h = jnp.arange(LANES)[:, None]
    lane = jnp.arange(LANES)[None, :]
    sel = jnp.stack([jnp.where((h < heads) & (lane == 64 + 3 * h + k), -1.0, 0.0) for k in range(3)])
    ones = jnp.where((lane >= C_LANE) & (lane < C_LANE + 3), 1.0, 0.0)
    return sel.astype(BF16), ones.astype(F32)


def kernel(x, norm_g, final_g, lru_w_in, lru_conv_w, lru_conv_b, lru_wa, lru_ba, lru_wx, lru_bx,
           lru_a_param, lru_w_out, fox_w_in, fox_b_f, fox_w_out):
    assert norm_g.shape[0] == 2 and lru_w_in.shape[0] == 1 and fox_w_in.shape[0] == 1
    d = x.shape[-1]
    width = lru_w_out.shape[1]
    n_blocks, blk = lru_wa.shape[1], lru_wa.shape[2]
    heads = fox_b_f.shape[1]
    fw = fox_w_out.shape[1]
    head_dim = fw // heads
    assert head_dim == 64 and heads % 2 == 0 and 64 + 3 * heads <= C_LANE

    wg = jnp.concatenate([lru_wa[0], lru_wx[0]], axis=-1).astype(BF16)
    bg = jnp.concatenate([lru_ba[0].reshape(n_blocks, 1, blk), lru_bx[0].reshape(n_blocks, 1, blk)], axis=-1)
    x1 = _lru_layer(x, norm_g[0:1], lru_w_in[0].astype(BF16), lru_conv_w[0], lru_conv_b[0:1], wg, bg,
                    lru_a_param[0:1], lru_w_out[0].astype(BF16), n_blocks)

    w = fox_w_in[0]
    wqt = w[:, 0 * fw:1 * fw].T.astype(BF16)
    wk = w[:, 1 * fw:2 * fw].astype(BF16)
    wvt = w[:, 2 * fw:3 * fw].T.astype(BF16)
    wgt = w[:, 3 * fw:4 * fw].T.astype(BF16)
    wft = jnp.pad(w[:, 4 * fw:].T, ((0, LANES - heads), (0, 0))).astype(BF16)
    bft = jnp.broadcast_to(jnp.pad(fox_b_f[0], (0, LANES - heads))[:, None], (LANES, LANES))
    sel, ones = _key_aug_constants(heads)
    ka, qt, vt, sgt, ct = _fox_in(x1, norm_g[1:2], wqt, wk, wvt, wgt, wft, bft, sel, ones,
                                  heads, head_dim)
    yt = _fox_attn(qt, ct, ka, vt, sgt, heads, head_dim)
    return _fox_out(yt, x1, fox_w_out[0].astype(BF16), final_g.reshape(1, d))
```

```python
import functools

import jax
import jax.numpy as jnp
from jax import lax
from jax.experimental import pallas as pl
from jax.experimental.pallas import tpu as pltpu

F32 = jnp.float32
BF16 = jnp.bfloat16

EPS = 1e-6
LRU_C = 8.0
CONV_WIDTH = 4
NEG_INF = -1e30
LOG2E = 1.4426950408889634

LANES = 128
SUBLANES = 8
VMEM_LIMIT_BYTES = 56 * 1024 * 1024

LRU_TILE = 512
FOX_TILE = 512
ATTN_TQ = 2048
ATTN_TK = 256
ATTN_COLS = 512

C_LANE = 112


def _rmsnorm(x, g):
    return x * lax.rsqrt(jnp.mean(x * x, axis=-1, keepdims=True) + EPS) * g


def _sigmoid(x):
    return 0.5 + 0.5 * jnp.tanh(0.5 * x)


def _log_sigmoid(x):
    return jnp.minimum(x, 0.0) - jnp.log1p(jnp.exp(-jnp.abs(x)))


def _split3(c):
    c1 = c.astype(BF16)
    r1 = c - c1.astype(F32)
    c2 = r1.astype(BF16)
    c3 = (r1 - c2.astype(F32)).astype(BF16)
    return c1, c2, c3


def _dot(a, b):
    return jnp.dot(a, b, preferred_element_type=F32)


def _dot_nt(a, b):
    return lax.dot_general(a, b, (((1,), (1,)), ((), ())), preferred_element_type=F32)


def _lru_kernel(x_ref, g_ref, win_ref, cw_ref, cb_ref, wg_ref, bg_ref, ap_ref, wout_ref, o_ref,
                ext_ref, a_ref, b_ref, hs_ref, hcar_ref, *, tile, width, n_blocks):
    t = pl.program_id(1)
    blk = width // n_blocks

    @pl.when(t == 0)
    def _():
        ext_ref[0:SUBLANES, :] = jnp.zeros((SUBLANES, width), F32)
        hcar_ref[...] = jnp.zeros_like(hcar_ref)

    x = x_ref[...]
    xn = _rmsnorm(x, g_ref[...]).astype(BF16)
    u = _dot(xn, win_ref[...])
    gate = u[:, width:]
    ext_ref[SUBLANES:SUBLANES + tile, :] = u[:, :width]

    cw = cw_ref[...]
    xc = cb_ref[...] + cw[3:4] * ext_ref[SUBLANES:SUBLANES + tile, :]
    for k in range(CONV_WIDTH - 1):
        off = SUBLANES - (CONV_WIDTH - 1) + k
        xc = xc + cw[k:k + 1] * ext_ref[off:off + tile, :]
    ext_ref[0:SUBLANES, :] = ext_ref[tile:tile + SUBLANES, :]

    ap = ap_ref[...]
    sp = jnp.maximum(-ap, 0.0) + jnp.log1p(jnp.exp(-jnp.abs(ap)))
    xcb = xc.astype(BF16)
    for n in range(n_blocks):
        sl = slice(n * blk, (n + 1) * blk)
        g = _dot(xcb[:, sl], wg_ref[n]) + bg_ref[n]
        r = _sigmoid(g[:, :blk])
        i = _sigmoid(g[:, blk:])
        neg_log_a = LRU_C * r * sp[:, sl]
        a = jnp.exp(-neg_log_a)
        a_ref[:, sl] = a
        one_minus_a2 = jnp.tanh(neg_log_a) * (1.0 + a * a)
        mult = jnp.where(one_minus_a2 > 0.0, one_minus_a2 * lax.rsqrt(one_minus_a2), 0.0)
        b_ref[:, sl] = mult * (i * xc[:, sl])


    def step(i, h):
        h = a_ref[pl.ds(i, 1), :] * h + b_ref[pl.ds(i, 1), :]
        hs_ref[pl.ds(i, 1), :] = h
        return h

    h = lax.fori_loop(0, tile, step, hcar_ref[0:1, :], unroll=SUBLANES)
    hcar_ref[0:1, :] = h

    y = hs_ref[...] * (gate * _sigmoid(gate))
    o_ref[...] = x + _dot(y.astype(BF16), wout_ref[...])


def _lru_layer(x, g, w_in, conv_w, conv_b, wg, bg, a_param, w_out, n_blocks):
    bsz, s, d = x.shape
    width = w_out.shape[0]
    tile = LRU_TILE
    kern = functools.partial(_lru_kernel, tile=tile, width=width, n_blocks=n_blocks)
    const = lambda *shape: pl.BlockSpec(shape, lambda b, t: (0,) * len(shape))
    return pl.pallas_call(
        kern,
        grid=(bsz, s // tile),
        in_specs=[
            pl.BlockSpec((None, tile, d), lambda b, t: (b, t, 0)),
            const(1, d),
            const(d, 2 * width),
            const(CONV_WIDTH, width),
            const(1, width),
            const(n_blocks, width // n_blocks, 2 * (width // n_blocks)),
            const(n_blocks, 1, 2 * (width // n_blocks)),
            const(1, width),
            const(width, d),
        ],
        out_specs=pl.BlockSpec((None, tile, d), lambda b, t: (b, t, 0)),
        out_shape=jax.ShapeDtypeStruct((bsz, s, d), F32),
        scratch_shapes=[
            pltpu.VMEM((tile + SUBLANES, width), F32),
            pltpu.VMEM((tile, width), F32),
            pltpu.VMEM((tile, width), F32),
            pltpu.VMEM((tile, width), F32),
            pltpu.VMEM((SUBLANES, width), F32),
        ],
        compiler_params=pltpu.CompilerParams(
            dimension_semantics=("arbitrary", "arbitrary"), vmem_limit_bytes=VMEM_LIMIT_BYTES),
        name="lru_layer",
    )(x, g, w_in, conv_w, conv_b, wg, bg, a_param, w_out)


def _fox_in_kernel(x_ref, g_ref, wqt_ref, wk_ref, wvt_ref, wgt_ref, wft_ref, bft_ref,
                   sel_ref, ones_ref, ka_ref, qt_ref, vt_ref, sgt_ref, ct_ref, cart_ref,
                   *, tile, heads, head_dim, scale):
    t = pl.program_id(1)

    @pl.when(t == 0)
    def _():
        cart_ref[...] = jnp.zeros_like(cart_ref)

    hn = _rmsnorm(x_ref[...], g_ref[...]).astype(BF16)

    qt_ref[...] = (_dot_nt(wqt_ref[...], hn) * scale).astype(BF16)
    vt_ref[...] = _dot_nt(wvt_ref[...], hn).astype(BF16)
    gt = _dot_nt(wgt_ref[...], hn)
    sgt_ref[...] = (gt * _sigmoid(gt)).astype(BF16)

    row = lax.broadcasted_iota(jnp.int32, (tile, tile), 0)
    col = lax.broadcasted_iota(jnp.int32, (tile, tile), 1)
    tri_upper = (row <= col).astype(BF16)

    lft = _log_sigmoid(_dot_nt(wft_ref[...], hn) + bft_ref[...][:, 0:1])
    hrow = lax.broadcasted_iota(jnp.int32, lft.shape, 0)
    lft = jnp.where(hrow < heads, lft, 0.0)
    p1, p2, p3 = _split3(lft)
    ct = _dot(p1, tri_upper) + _dot(p2, tri_upper) + _dot(p3, tri_upper) + cart_ref[...][:, 0:1]
    cart_ref[...] = jnp.broadcast_to(ct[:, tile - 1:tile], cart_ref.shape)
    ct_ref[...] = ct[0:heads, :]

    c1, c2, c3 = _split3(jnp.transpose(ct) * LOG2E)
    aug = _dot(c1, sel_ref[0]) + _dot(c2, sel_ref[1]) + _dot(c3, sel_ref[2]) + ones_ref[...]

    k = _dot(hn, wk_ref[...])
    lane = lax.broadcasted_iota(jnp.int32, (tile, LANES), 1)
    is_key = lane < head_dim
    for p in range(heads // 2):
        slab = k[:, p * LANES:(p + 1) * LANES]
        ka_ref[2 * p] = jnp.where(is_key, slab, aug).astype(BF16)
        ka_ref[2 * p + 1] = jnp.where(is_key, pltpu.roll(slab, head_dim, 1), aug).astype(BF16)


def _fox_in(x, g, wqt, wk, wvt, wgt, wft, bft, sel, ones, heads, head_dim):
    bsz, s, d = x.shape
    fw = heads * head_dim
    tile = FOX_TILE
    kern = functools.partial(_fox_in_kernel, tile=tile, heads=heads, head_dim=head_dim,
                             scale=LOG2E / (head_dim ** 0.5))
    const = lambda *shape: pl.BlockSpec(shape, lambda b, t: (0,) * len(shape))
    feat_major = pl.BlockSpec((None, fw, tile), lambda b, t: (b, 0, t))
    return pl.pallas_call(
        kern,
        grid=(bsz, s // tile),
        in_specs=[
            pl.BlockSpec((None, tile, d), lambda b, t: (b, t, 0)),
            const(1, d),
            const(fw, d), const(d, fw), const(fw, d), const(fw, d),
            const(LANES, d), const(LANES, LANES),
            const(3, LANES, LANES), const(1, LANES),
        ],
        out_specs=[
            pl.BlockSpec((None, heads, tile, LANES), lambda b, t: (b, 0, t, 0)),
            feat_major, feat_major, feat_major,
            pl.BlockSpec((None, heads, tile), lambda b, t: (b, 0, t)),
        ],
        out_shape=[
            jax.ShapeDtypeStruct((bsz, heads, s, LANES), BF16),
            jax.ShapeDtypeStruct((bsz, fw, s), BF16),
            jax.ShapeDtypeStruct((bsz, fw, s), BF16),
            jax.ShapeDtypeStruct((bsz, fw, s), BF16),
            jax.ShapeDtypeStruct((bsz, heads, s), F32),
        ],
        scratch_shapes=[
            pltpu.VMEM((LANES, LANES), F32),
        ],
        compiler_params=pltpu.CompilerParams(
            dimension_semantics=("arbitrary", "arbitrary"), vmem_limit_bytes=VMEM_LIMIT_BYTES),
        name="fox_in",
    )(x, g, wqt, wk, wvt, wgt, wft, bft, sel, ones)


def _attn_kernel(qt_ref, ct_ref, ka_ref, vt_ref, sgt_ref, yt_ref, qa_ref, vx_ref, s0_ref, s1_ref,
                 acc_ref, *, tq, tk, head_dim, n_kv):
    h = pl.program_id(1)
    qi = pl.program_id(2)
    ones_rows = vx_ref.shape[1] - head_dim
    n_sub = tq // tk
    assert n_sub * tk == tq and n_sub % 2 == 0

    @pl.when(qi == 0)
    def _():
        for j in range(n_kv):
            vx_ref[j, 0:head_dim, :] = vt_ref[:, j * tk:(j + 1) * tk]
            vx_ref[j, head_dim:, :] = jnp.ones((ones_rows, tk), BF16)

    qa_ref[0:head_dim, :] = qt_ref[...]
    c1, c2, c3 = _split3(ct_ref[pl.ds(h, 1), :] * LOG2E)
    arow = lax.broadcasted_iota(jnp.int32, (LANES - head_dim, tq), 0)
    aug = jnp.where((arow >= 3 * h) & (arow < 3 * h + 3), 1.0, 0.0)
    cbase = C_LANE - head_dim
    aug = jnp.where(arow == cbase, c1.astype(F32), aug)
    aug = jnp.where(arow == cbase + 1, c2.astype(F32), aug)
    aug = jnp.where(arow == cbase + 2, c3.astype(F32), aug)
    qa_ref[head_dim:, :] = aug.astype(BF16)

    def scores(j, qlo=0):
        start = pl.multiple_of(j * tk, tk)
        return _dot(ka_ref[pl.ds(start, tk), :], qa_ref[:, qlo:])

    def update(s_ref, j, m_all, qlo=None):
        lo = qlo or 0
        m_parts = [m_all[:, :lo]] if lo else []
        for c0 in range(lo, tq, ATTN_COLS):
            c1 = min(c0 + ATTN_COLS, tq)
            cs = c0 - lo
            s = s_ref[:, cs:c1 - lo]
            if qlo is not None and cs < tk:
                kr = lax.broadcasted_iota(jnp.int32, s.shape, 0)
                qc = lax.broadcasted_iota(jnp.int32, s.shape, 1) + cs
                s = jnp.where(kr <= qc, s, NEG_INF)
            m = m_all[:, c0:c1]
            m_new = jnp.maximum(m, jnp.max(s, axis=0, keepdims=True))
            alpha = jnp.exp2(m - m_new)
            p = jnp.exp2(s - m_new).astype(BF16)
            acc_ref[:, c0:c1] = acc_ref[:, c0:c1] * alpha + _dot(vx_ref[j], p)
            m_parts.append(m_new)
        return jnp.concatenate(m_parts, axis=1)

    s_refs = (s0_ref, s1_ref)

    def group(i, carry, diagonal):
        for u in range(n_sub):
            j = n_sub * i + u
            if not (diagonal and u == n_sub - 1):
                qlo_next = (u + 1) * tk if diagonal else 0
                s_refs[(u + 1) % 2][:, 0:tq - qlo_next] = scores(j + 1, qlo_next)
            qlo = u * tk if diagonal else 0
            carry = update(s_refs[u % 2], j, carry, qlo=qlo if diagonal else None)
        return carry

    m0 = jnp.full((1, tq), NEG_INF, F32)
    acc_ref[...] = jnp.zeros_like(acc_ref)
    s0_ref[...] = scores(0)
    m = lax.fori_loop(0, qi, lambda i, c: group(i, c, False), m0)
    group(qi, m, True)

    o = acc_ref[0:head_dim, :] / acc_ref[head_dim:head_dim + 1, :]
    yt_ref[...] = (o * sgt_ref[...].astype(F32)).astype(BF16)


def _fox_attn(qt, ct, ka, vt, sgt, heads, head_dim):
    bsz, fw, s = qt.shape
    tq, tk = ATTN_TQ, ATTN_TK
    n_kv = s // tk
    kern = functools.partial(_attn_kernel, tq=tq, tk=tk, head_dim=head_dim, n_kv=n_kv)
    q_block = pl.BlockSpec((None, head_dim, tq), lambda b, h, q: (b, h, q))
    return pl.pallas_call(
        kern,
        grid=(bsz, heads, s // tq),
        in_specs=[
            q_block,
            pl.BlockSpec((None, heads, tq), lambda b, h, q: (b, 0, q)),
            pl.BlockSpec((None, None, s, LANES), lambda b, h, q: (b, h, 0, 0)),
            pl.BlockSpec((None, head_dim, s), lambda b, h, q: (b, h, 0)),
            q_block,
        ],
        out_specs=q_block,
        out_shape=jax.ShapeDtypeStruct((bsz, fw, s), BF16),
        scratch_shapes=[
            pltpu.VMEM((LANES, tq), BF16),
            pltpu.VMEM((n_kv, head_dim + 16, tk), BF16),
            pltpu.VMEM((tk, tq), F32),
            pltpu.VMEM((tk, tq), F32),
            pltpu.VMEM((head_dim + 16, tq), F32),
        ],
        compiler_params=pltpu.CompilerParams(
            dimension_semantics=("arbitrary", "arbitrary", "arbitrary"),
            vmem_limit_bytes=VMEM_LIMIT_BYTES),
        name="fox_attn",
    )(qt, ct, ka, vt, sgt)


def _fox_out_kernel(yt_ref, x_ref, wo_ref, g_ref, o_ref):
    y = jnp.transpose(yt_ref[...].astype(F32)).astype(BF16)
    z = x_ref[...] + _dot(y, wo_ref[...])
    o_ref[...] = _rmsnorm(z, g_ref[...])


def _fox_out(yt, x, wo, g):
    bsz, s, d = x.shape
    fw = yt.shape[1]
    tile = FOX_TILE
    return pl.pallas_call(
        _fox_out_kernel,
        grid=(bsz, s // tile),
        in_specs=[
            pl.BlockSpec((None, fw, tile), lambda b, t: (b, 0, t)),
            pl.BlockSpec((None, tile, d), lambda b, t: (b, t, 0)),
            pl.BlockSpec((fw, d), lambda b, t: (0, 0)),
            pl.BlockSpec((1, d), lambda b, t: (0, 0)),
        ],
        out_specs=pl.BlockSpec((None, tile, d), lambda b, t: (b, t, 0)),
        out_shape=jax.ShapeDtypeStruct((bsz, s, d), F32),
        compiler_params=pltpu.CompilerParams(
            dimension_semantics=("arbitrary", "arbitrary"), vmem_limit_bytes=VMEM_LIMIT_BYTES),
        name="fox_out",
    )(yt, x, wo, g)


def _key_aug_constants(heads):
    h = jnp.arange(LANES)[:, None]
    lane = jnp.arange(LANES)[None, :]
    sel = jnp.stack([jnp.where((h < heads) & (lane == 64 + 3 * h + k), -1.0, 0.0) for k in range(3)])
    ones = jnp.where((lane >= C_LANE) & (lane < C_LANE + 3), 1.0, 0.0)
    return sel.astype(BF16), ones.astype(F32)


def kernel(x, norm_g, final_g, lru_w_in, lru_conv_w, lru_conv_b, lru_wa, lru_ba, lru_wx, lru_bx,
           lru_a_param, lru_w_out, fox_w_in, fox_b_f, fox_w_out):
    assert norm_g.shape[0] == 2 and lru_w_in.shape[0] == 1 and fox_w_in.shape[0] == 1
    d = x.shape[-1]
    width = lru_w_out.shape[1]
    n_blocks, blk = lru_wa.shape[1], lru_wa.shape[2]
    heads = fox_b_f.shape[1]
    fw = fox_w_out.shape[1]
    head_dim = fw // heads
    assert head_dim == 64 and heads % 2 == 0 and 64 + 3 * heads <= C_LANE

    wg = jnp.concatenate([lru_wa[0], lru_wx[0]], axis=-1).astype(BF16)
    bg = jnp.concatenate([lru_ba[0].reshape(n_blocks, 1, blk), lru_bx[0].reshape(n_blocks, 1, blk)], axis=-1)
    x1 = _lru_layer(x, norm_g[0:1], lru_w_in[0].astype(BF16), lru_conv_w[0], lru_conv_b[0:1], wg, bg,
                    lru_a_param[0:1], lru_w_out[0].astype(BF16), n_blocks)

    w = fox_w_in[0]
    wqt = w[:, 0 * fw:1 * fw].T.astype(BF16)
    wk = w[:, 1 * fw:2 * fw].astype(BF16)
    wvt = w[:, 2 * fw:3 * fw].T.astype(BF16)
    wgt = w[:, 3 * fw:4 * fw].T.astype(BF16)
    wft = jnp.pad(w[:, 4 * fw:].T, ((0, LANES - heads), (0, 0))).astype(BF16)
    bft = jnp.broadcast_to(jnp.pad(fox_b_f[0], (0, LANES - heads))[:, None], (LANES, LANES))
    sel, ones = _key_aug_constants(heads)
    ka, qt, vt, sgt, ct = _fox_in(x1, norm_g[1:2], wqt, wk, wvt, wgt, wft, bft, sel, ones,
                                  heads, head_dim)
    yt = _fox_attn(qt, ct, ka, vt, sgt, heads, head_dim)
    return _fox_out(yt, x1, fox_w_out[0].astype(BF16), final_g.reshape(1, d))
```

```python
import functools

import jax
import jax.numpy as jnp
from jax import lax
from jax.experimental import pallas as pl
from jax.experimental.pallas import tpu as pltpu

F32 = jnp.float32
BF16 = jnp.bfloat16

EPS = 1e-6
LRU_C = 8.0
CONV_WIDTH = 4
NEG_INF = -1e30
LOG2E = 1.4426950408889634

LANES = 128
SUBLANES = 8
VMEM_LIMIT_BYTES = 56 * 1024 * 1024

LRU_TILE = 512
FOX_TILE = 512
OUT_TILE = 1024
ATTN_TQ = 2048
ATTN_TK = 256
ATTN_HEADS = 2
ATTN_COLS = 512

C_LANE = 112


def _rmsnorm(x, g):
    return x * lax.rsqrt(jnp.mean(x * x, axis=-1, keepdims=True) + EPS) * g


def _sigmoid(x):
    return 0.5 + 0.5 * jnp.tanh(0.5 * x)


def _log_sigmoid(x):
    return jnp.minimum(x, 0.0) - jnp.log1p(jnp.exp(-jnp.abs(x)))


def _split3(c):
    c1 = c.astype(BF16)
    r1 = c - c1.astype(F32)
    c2 = r1.astype(BF16)
    c3 = (r1 - c2.astype(F32)).astype(BF16)
    return c1, c2, c3


def _dot(a, b):
    return jnp.dot(a, b, preferred_element_type=F32)


def _dot_nt(a, b):
    return lax.dot_general(a, b, (((1,), (1,)), ((), ())), preferred_element_type=F32)


def _lru_kernel(x_ref, g_ref, win_ref, cw_ref, cb_ref, wg_ref, bg_ref, ap_ref, wout_ref, o_ref,
                ext_ref, a_ref, b_ref, hs_ref, hcar_ref, *, tile, width, n_blocks):
    t = pl.program_id(1)
    blk = width // n_blocks

    @pl.when(t == 0)
    def _():
        ext_ref[0:SUBLANES, :] = jnp.zeros((SUBLANES, width), F32)
        hcar_ref[...] = jnp.zeros_like(hcar_ref)

    x = x_ref[...]
    xn = _rmsnorm(x, g_ref[...]).astype(BF16)
    u = _dot(xn, win_ref[...])
    gate = u[:, width:]
    ext_ref[SUBLANES:SUBLANES + tile, :] = u[:, :width]

    cw = cw_ref[...]
    xc = cb_ref[...] + cw[3:4] * ext_ref[SUBLANES:SUBLANES + tile, :]
    for k in range(CONV_WIDTH - 1):
        off = SUBLANES - (CONV_WIDTH - 1) + k
        xc = xc + cw[k:k + 1] * ext_ref[off:off + tile, :]
    ext_ref[0:SUBLANES, :] = ext_ref[tile:tile + SUBLANES, :]

    ap = ap_ref[...]
    sp = jnp.maximum(-ap, 0.0) + jnp.log1p(jnp.exp(-jnp.abs(ap)))
    xcb = xc.astype(BF16)
    for n in range(n_blocks):
        sl = slice(n * blk, (n + 1) * blk)
        g = _dot(xcb[:, sl], wg_ref[n]) + bg_ref[n]
        r = _sigmoid(g[:, :blk])
        i = _sigmoid(g[:, blk:])
        neg_log_a = LRU_C * r * sp[:, sl]
        a = jnp.exp(-neg_log_a)
        a_ref[:, sl] = a
        one_minus_a2 = jnp.tanh(neg_log_a) * (1.0 + a * a)
        mult = jnp.where(one_minus_a2 > 0.0, one_minus_a2 * lax.rsqrt(one_minus_a2), 0.0)
        b_ref[:, sl] = mult * (i * xc[:, sl])


    def step(i, h):
        h = a_ref[pl.ds(i, 1), :] * h + b_ref[pl.ds(i, 1), :]
        hs_ref[pl.ds(i, 1), :] = h
        return h

    h = lax.fori_loop(0, tile, step, hcar_ref[0:1, :], unroll=SUBLANES)
    hcar_ref[0:1, :] = h

    y = hs_ref[...] * (gate * _sigmoid(gate))
    o_ref[...] = x + _dot(y.astype(BF16), wout_ref[...])


def _lru_layer(x, g, w_in, conv_w, conv_b, wg, bg, a_param, w_out, n_blocks):
    bsz, s, d = x.shape
    width = w_out.shape[0]
    tile = LRU_TILE
    kern = functools.partial(_lru_kernel, tile=tile, width=width, n_blocks=n_blocks)
    const = lambda *shape: pl.BlockSpec(shape, lambda b, t: (0,) * len(shape))
    return pl.pallas_call(
        kern,
        grid=(bsz, s // tile),
        in_specs=[
            pl.BlockSpec((None, tile, d), lambda b, t: (b, t, 0)),
            const(1, d),
            const(d, 2 * width),
            const(CONV_WIDTH, width),
            const(1, width),
            const(n_blocks, width // n_blocks, 2 * (width // n_blocks)),
            const(n_blocks, 1, 2 * (width // n_blocks)),
            const(1, width),
            const(width, d),
        ],
        out_specs=pl.BlockSpec((None, tile, d), lambda b, t: (b, t, 0)),
        out_shape=jax.ShapeDtypeStruct((bsz, s, d), F32),
        scratch_shapes=[
            pltpu.VMEM((tile + SUBLANES, width), F32),
            pltpu.VMEM((tile, width), F32),
            pltpu.VMEM((tile, width), F32),
            pltpu.VMEM((tile, width), F32),
            pltpu.VMEM((SUBLANES, width), F32),
        ],
        compiler_params=pltpu.CompilerParams(
            dimension_semantics=("arbitrary", "arbitrary"), vmem_limit_bytes=VMEM_LIMIT_BYTES),
        name="lru_layer",
    )(x, g, w_in, conv_w, conv_b, wg, bg, a_param, w_out)


def _fox_in_kernel(x_ref, g_ref, wqt_ref, wk_ref, wvt_ref, wgt_ref, wft_ref, bft_ref,
                   sel_ref, ones_ref, ka_ref, qt_ref, vt_ref, sgt_ref, ct_ref, cart_ref,
                   *, tile, heads, head_dim, scale):
    t = pl.program_id(1)

    @pl.when(t == 0)
    def _():
        cart_ref[...] = jnp.zeros_like(cart_ref)

    hn = _rmsnorm(x_ref[...], g_ref[...]).astype(BF16)

    qt_ref[...] = (_dot_nt(wqt_ref[...], hn) * scale).astype(BF16)
    vt_ref[...] = _dot_nt(wvt_ref[...], hn).astype(BF16)
    gt = _dot_nt(wgt_ref[...], hn)
    sgt_ref[...] = (gt * _sigmoid(gt)).astype(BF16)

    row = lax.broadcasted_iota(jnp.int32, (tile, tile), 0)
    col = lax.broadcasted_iota(jnp.int32, (tile, tile), 1)
    tri_upper = (row <= col).astype(BF16)

    lft = _log_sigmoid(_dot_nt(wft_ref[...], hn) + bft_ref[...][:, 0:1])
    hrow = lax.broadcasted_iota(jnp.int32, lft.shape, 0)
    lft = jnp.where(hrow < heads, lft, 0.0)
    p1, p2, p3 = _split3(lft)
    ct = _dot(p1, tri_upper) + _dot(p2, tri_upper) + _dot(p3, tri_upper) + cart_ref[...][:, 0:1]
    cart_ref[...] = jnp.broadcast_to(ct[:, tile - 1:tile], cart_ref.shape)
    ct_ref[...] = ct[0:heads, :]

    c1, c2, c3 = _split3(jnp.transpose(ct) * LOG2E)
    aug = _dot(c1, sel_ref[0]) + _dot(c2, sel_ref[1]) + _dot(c3, sel_ref[2]) + ones_ref[...]

    k = _dot(hn, wk_ref[...])
    lane = lax.broadcasted_iota(jnp.int32, (tile, LANES), 1)
    is_key = lane < head_dim
    for p in range(heads // 2):
        slab = k[:, p * LANES:(p + 1) * LANES]
        ka_ref[2 * p] = jnp.where(is_key, slab, aug).astype(BF16)
        ka_ref[2 * p + 1] = jnp.where(is_key, pltpu.roll(slab, head_dim, 1), aug).astype(BF16)


def _fox_in(x, g, wqt, wk, wvt, wgt, wft, bft, sel, ones, heads, head_dim):
    bsz, s, d = x.shape
    fw = heads * head_dim
    tile = FOX_TILE
    kern = functools.partial(_fox_in_kernel, tile=tile, heads=heads, head_dim=head_dim,
                             scale=LOG2E / (head_dim ** 0.5))
    const = lambda *shape: pl.BlockSpec(shape, lambda b, t: (0,) * len(shape))
    feat_major = pl.BlockSpec((None, fw, tile), lambda b, t: (b, 0, t))
    return pl.pallas_call(
        kern,
        grid=(bsz, s // tile),
        in_specs=[
            pl.BlockSpec((None, tile, d), lambda b, t: (b, t, 0)),
            const(1, d),
            const(fw, d), const(d, fw), const(fw, d), const(fw, d),
            const(LANES, d), const(LANES, LANES),
            const(3, LANES, LANES), const(1, LANES),
        ],
        out_specs=[
            pl.BlockSpec((None, heads, tile, LANES), lambda b, t: (b, 0, t, 0)),
            feat_major, feat_major, feat_major,
            pl.BlockSpec((None, heads, tile), lambda b, t: (b, 0, t)),
        ],
        out_shape=[
            jax.ShapeDtypeStruct((bsz, heads, s, LANES), BF16),
            jax.ShapeDtypeStruct((bsz, fw, s), BF16),
            jax.ShapeDtypeStruct((bsz, fw, s), BF16),
            jax.ShapeDtypeStruct((bsz, fw, s), BF16),
            jax.ShapeDtypeStruct((bsz, heads, s), F32),
        ],
        scratch_shapes=[
            pltpu.VMEM((LANES, LANES), F32),
        ],
        compiler_params=pltpu.CompilerParams(
            dimension_semantics=("arbitrary", "arbitrary"), vmem_limit_bytes=VMEM_LIMIT_BYTES),
        name="fox_in",
    )(x, g, wqt, wk, wvt, wgt, wft, bft, sel, ones)


def _attn_kernel(qt_ref, ct_ref, ka_ref, vt_ref, sgt_ref, yt_ref, qa_ref, vx_ref, s_ref, acc_ref,
                 *, head_dim, **static):
    qi = pl.program_id(2)
    tk, n_kv = static["tk"], static["n_kv"]

    @pl.when(qi == 0)
    def _():
        for hh in range(ATTN_HEADS):
            for j in range(n_kv):
                vx_ref[hh, j, 0:head_dim, :] = vt_ref[hh * head_dim:(hh + 1) * head_dim, j * tk:(j + 1) * tk]
                vx_ref[hh, j, head_dim:, :] = jnp.ones((vx_ref.shape[2] - head_dim, tk), BF16)

    for hh in range(ATTN_HEADS):
        rows = slice(hh * head_dim, (hh + 1) * head_dim)
        _attn_head(pl.program_id(1) * ATTN_HEADS + hh, qi, qt_ref.at[rows], ct_ref, ka_ref.at[hh],
                   sgt_ref.at[rows], yt_ref.at[rows], qa_ref.at[hh], vx_ref.at[hh],
                   s_ref.at[hh, 0], s_ref.at[hh, 1], acc_ref.at[hh], head_dim=head_dim, **static)


def _attn_head(h, qi, qt_ref, ct_ref, ka_ref, sgt_ref, yt_ref, qa_ref, vx_ref, s0_ref, s1_ref,
               acc_ref, *, tq, tk, head_dim, n_kv):
    n_sub = tq // tk
    assert n_sub * tk == tq and n_sub % 2 == 0

    qa_ref[0:head_dim, :] = qt_ref[...]
    c1, c2, c3 = _split3(ct_ref[pl.ds(h, 1), :] * LOG2E)
    arow = lax.broadcasted_iota(jnp.int32, (LANES - head_dim, tq), 0)
    aug = jnp.where((arow >= 3 * h) & (arow < 3 * h + 3), 1.0, 0.0)
    cbase = C_LANE - head_dim
    aug = jnp.where(arow == cbase, c1.astype(F32), aug)
    aug = jnp.where(arow == cbase + 1, c2.astype(F32), aug)
    aug = jnp.where(arow == cbase + 2, c3.astype(F32), aug)
    qa_ref[head_dim:, :] = aug.astype(BF16)

    def scores(j, qlo=0):
        start = pl.multiple_of(j * tk, tk)
        return _dot(ka_ref[pl.ds(start, tk), :], qa_ref[:, qlo:])

    def update(s_ref, j, m_all, qlo=None):
        lo = qlo or 0
        m_parts = [m_all[:, :lo]] if lo else []
        for c0 in range(lo, tq, ATTN_COLS):
            c1 = min(c0 + ATTN_COLS, tq)
            cs = c0 - lo
            s = s_ref[:, cs:c1 - lo]
            if qlo is not None and cs < tk:
                kr = lax.broadcasted_iota(jnp.int32, s.shape, 0)
                qc = lax.broadcasted_iota(jnp.int32, s.shape, 1) + cs
                s = jnp.where(kr <= qc, s, NEG_INF)
            m = m_all[:, c0:c1]
            m_new = jnp.maximum(m, jnp.max(s, axis=0, keepdims=True))
            alpha = jnp.exp2(m - m_new)
            p = jnp.exp2(s - m_new).astype(BF16)
            acc_ref[:, c0:c1] = acc_ref[:, c0:c1] * alpha + _dot(vx_ref[j], p)
            m_parts.append(m_new)
        return jnp.concatenate(m_parts, axis=1)

    s_refs = (s0_ref, s1_ref)

    def group(i, carry, diagonal):
        for u in range(n_sub):
            j = n_sub * i + u
            if not (diagonal and u == n_sub - 1):
                qlo_next = (u + 1) * tk if diagonal else 0
                s_refs[(u + 1) % 2][:, 0:tq - qlo_next] = scores(j + 1, qlo_next)
            qlo = u * tk if diagonal else 0
            carry = update(s_refs[u % 2], j, carry, qlo=qlo if diagonal else None)
        return carry

    m0 = jnp.full((1, tq), NEG_INF, F32)
    acc_ref[...] = jnp.zeros_like(acc_ref)
    s0_ref[...] = scores(0)
    m = lax.fori_loop(0, qi, lambda i, c: group(i, c, False), m0)
    group(qi, m, True)

    o = acc_ref[0:head_dim, :] / acc_ref[head_dim:head_dim + 1, :]
    yt_ref[...] = (o * sgt_ref[...].astype(F32)).astype(BF16)


def _fox_attn(qt, ct, ka, vt, sgt, heads, head_dim):
    bsz, fw, s = qt.shape
    tq, tk = ATTN_TQ, ATTN_TK
    n_kv = s // tk
    kern = functools.partial(_attn_kernel, tq=tq, tk=tk, head_dim=head_dim, n_kv=n_kv)
    hps = ATTN_HEADS
    assert heads % hps == 0
    q_block = pl.BlockSpec((None, hps * head_dim, tq), lambda b, h, q: (b, h, q))
    return pl.pallas_call(
        kern,
        grid=(bsz, heads // hps, s // tq),
        in_specs=[
            q_block,
            pl.BlockSpec((None, heads, tq), lambda b, h, q: (b, 0, q)),
            pl.BlockSpec((None, hps, s, LANES), lambda b, h, q: (b, h, 0, 0)),
            pl.BlockSpec((None, hps * head_dim, s), lambda b, h, q: (b, h, 0)),
            q_block,
        ],
        out_specs=q_block,
        out_shape=jax.ShapeDtypeStruct((bsz, fw, s), BF16),
        scratch_shapes=[
            pltpu.VMEM((hps, LANES, tq), BF16),
            pltpu.VMEM((hps, n_kv, head_dim + 16, tk), BF16),
            pltpu.VMEM((hps, 2, tk, tq), F32),
            pltpu.VMEM((hps, head_dim + 16, tq), F32),
        ],
        compiler_params=pltpu.CompilerParams(
            dimension_semantics=("arbitrary", "arbitrary", "arbitrary"),
            vmem_limit_bytes=VMEM_LIMIT_BYTES),
        name="fox_attn",
    )(qt, ct, ka, vt, sgt)


def _fox_out_kernel(yt_ref, x_ref, wo_ref, g_ref, o_ref):
    z = x_ref[...] + lax.dot_general(yt_ref[...], wo_ref[...], (((0,), (0,)), ((), ())),
                                     preferred_element_type=F32)
    o_ref[...] = _rmsnorm(z, g_ref[...])


def _fox_out(yt, x, wo, g):
    bsz, s, d = x.shape
    fw = yt.shape[1]
    tile = OUT_TILE
    return pl.pallas_call(
        _fox_out_kernel,
        grid=(bsz, s // tile),
        in_specs=[
            pl.BlockSpec((None, fw, tile), lambda b, t: (b, 0, t)),
            pl.BlockSpec((None, tile, d), lambda b, t: (b, t, 0)),
            pl.BlockSpec((fw, d), lambda b, t: (0, 0)),
            pl.BlockSpec((1, d), lambda b, t: (0, 0)),
        ],
        out_specs=pl.BlockSpec((None, tile, d), lambda b, t: (b, t, 0)),
        out_shape=jax.ShapeDtypeStruct((bsz, s, d), F32),
        compiler_params=pltpu.CompilerParams(
            dimension_semantics=("arbitrary", "arbitrary"), vmem_limit_bytes=VMEM_LIMIT_BYTES),
        name="fox_out",
    )(yt, x, wo, g)


def _key_aug_constants(heads):
    h = jnp.arange(LANES)[:, None]
    lane = jnp.arange(LANES)[None, :]
    sel = jnp.stack([jnp.where((h < heads) & (lane == 64 + 3 * h + k), -1.0, 0.0) for k in range(3)])
    ones = jnp.where((lane >= C_LANE) & (lane < C_LANE + 3), 1.0, 0.0)
    return sel.astype(BF16), ones.astype(F32)


def kernel(x, norm_g, final_g, lru_w_in, lru_conv_w, lru_conv_b, lru_wa, lru_ba, lru_wx, lru_bx,
           lru_a_param, lru_w_out, fox_w_in, fox_b_f, fox_w_out):
    assert norm_g.shape[0] == 2 and lru_w_in.shape[0] == 1 and fox_w_in.shape[0] == 1
    d = x.shape[-1]
    width = lru_w_out.shape[1]
    n_blocks, blk = lru_wa.shape[1], lru_wa.shape[2]
    heads = fox_b_f.shape[1]
    fw = fox_w_out.shape[1]
    head_dim = fw // heads
    assert head_dim == 64 and heads % 2 == 0 and 64 + 3 * heads <= C_LANE

    wg = jnp.concatenate([lru_wa[0], lru_wx[0]], axis=-1).astype(BF16)
    bg = jnp.concatenate([lru_ba[0].reshape(n_blocks, 1, blk), lru_bx[0].reshape(n_blocks, 1, blk)], axis=-1)
    x1 = _lru_layer(x, norm_g[0:1], lru_w_in[0].astype(BF16), lru_conv_w[0], lru_conv_b[0:1], wg, bg,
                    lru_a_param[0:1], lru_w_out[0].astype(BF16), n_blocks)

    w = fox_w_in[0]
    wqt = w[:, 0 * fw:1 * fw].T.astype(BF16)
    wk = w[:, 1 * fw:2 * fw].astype(BF16)
    wvt = w[:, 2 * fw:3 * fw].T.astype(BF16)
    wgt = w[:, 3 * fw:4 * fw].T.astype(BF16)
    wft = jnp.pad(w[:, 4 * fw:].T, ((0, LANES - heads), (0, 0))).astype(BF16)
    bft = jnp.broadcast_to(jnp.pad(fox_b_f[0], (0, LANES - heads))[:, None], (LANES, LANES))
    sel, ones = _key_aug_constants(heads)
    ka, qt, vt, sgt, ct = _fox_in(x1, norm_g[1:2], wqt, wk, wvt, wgt, wft, bft, sel, ones,
                                  heads, head_dim)
    yt = _fox_attn(qt, ct, ka, vt, sgt, heads, head_dim)
    return _fox_out(yt, x1, fox_w_out[0].astype(BF16), final_g.reshape(1, d))
```

```python
import functools

import jax
import jax.numpy as jnp
from jax import lax
from jax.experimental import pallas as pl
from jax.experimental.pallas import tpu as pltpu

F32 = jnp.float32
BF16 = jnp.bfloat16

EPS = 1e-6
LRU_C = 8.0
CONV_WIDTH = 4
NEG_INF = -1e30
LOG2E = 1.4426950408889634

LANES = 128
SUBLANES = 8
BF16_ROWS = 16
VMEM_LIMIT_BYTES = 56 * 1024 * 1024

LRU_TILE = 512
FOX_TILE = 512
OUT_TILE = 1024
ATTN_TQ = 2048
ATTN_TK = 256
ATTN_HEADS = 2
ATTN_COLS = 512

N_SPLIT = 3
C_LANE = 112


def _rmsnorm(x, g):
    return x * lax.rsqrt(jnp.mean(x * x, axis=-1, keepdims=True) + EPS) * g


def _sigmoid(x):
    return 0.5 + 0.5 * jnp.tanh(0.5 * x)


def _log_sigmoid(x):
    return jnp.minimum(x, 0.0) - jnp.log1p(jnp.exp(-jnp.abs(x)))


def _split3(c):
    c1 = c.astype(BF16)
    r1 = c - c1.astype(F32)
    c2 = r1.astype(BF16)
    c3 = (r1 - c2.astype(F32)).astype(BF16)
    return c1, c2, c3


def _dot(a, b):
    return jnp.dot(a, b, preferred_element_type=F32)


def _dot_nt(a, b):
    return lax.dot_general(a, b, (((1,), (1,)), ((), ())), preferred_element_type=F32)


def _lru_kernel(x_ref, g_ref, win_ref, cw_ref, cb_ref, wg_ref, bg_ref, ap_ref, wout_ref, o_ref,
                ext_ref, a_ref, b_ref, hs_ref, hcar_ref, *, tile, width, n_blocks):
    t = pl.program_id(1)
    blk = width // n_blocks

    @pl.when(t == 0)
    def _():
        ext_ref[0:SUBLANES, :] = jnp.zeros((SUBLANES, width), F32)
        hcar_ref[...] = jnp.zeros_like(hcar_ref)

    x = x_ref[...]
    xn = _rmsnorm(x, g_ref[...]).astype(BF16)
    u = _dot(xn, win_ref[...])
    gate = u[:, width:]
    ext_ref[SUBLANES:SUBLANES + tile, :] = u[:, :width]

    cw = cw_ref[...]
    xc = cb_ref[...] + cw[3:4] * ext_ref[SUBLANES:SUBLANES + tile, :]
    for k in range(CONV_WIDTH - 1):
        off = SUBLANES - (CONV_WIDTH - 1) + k
        xc = xc + cw[k:k + 1] * ext_ref[off:off + tile, :]
    ext_ref[0:SUBLANES, :] = ext_ref[tile:tile + SUBLANES, :]

    ap = ap_ref[...]
    sp = jnp.maximum(-ap, 0.0) + jnp.log1p(jnp.exp(-jnp.abs(ap)))
    xcb = xc.astype(BF16)
    for n in range(n_blocks):
        sl = slice(n * blk, (n + 1) * blk)
        g = _dot(xcb[:, sl], wg_ref[n]) + bg_ref[n]
        r = _sigmoid(g[:, :blk])
        i = _sigmoid(g[:, blk:])
        neg_log_a = LRU_C * r * sp[:, sl]
        a = jnp.exp(-neg_log_a)
        a_ref[:, sl] = a
        one_minus_a2 = jnp.tanh(neg_log_a) * (1.0 + a * a)
        mult = jnp.where(one_minus_a2 > 0.0, one_minus_a2 * lax.rsqrt(one_minus_a2), 0.0)
        b_ref[:, sl] = mult * (i * xc[:, sl])


    def step(i, h):
        h = a_ref[pl.ds(i, 1), :] * h + b_ref[pl.ds(i, 1), :]
        hs_ref[pl.ds(i, 1), :] = h
        return h

    h = lax.fori_loop(0, tile, step, hcar_ref[0:1, :], unroll=SUBLANES)
    hcar_ref[0:1, :] = h

    y = hs_ref[...] * (gate * _sigmoid(gate))
    o_ref[...] = x + _dot(y.astype(BF16), wout_ref[...])


def _lru_layer(x, g, w_in, conv_w, conv_b, wg, bg, a_param, w_out, n_blocks):
    bsz, s, d = x.shape
    width = w_out.shape[0]
    tile = LRU_TILE
    kern = functools.partial(_lru_kernel, tile=tile, width=width, n_blocks=n_blocks)
    const = lambda *shape: pl.BlockSpec(shape, lambda b, t: (0,) * len(shape))
    return pl.pallas_call(
        kern,
        grid=(bsz, s // tile),
        in_specs=[
            pl.BlockSpec((None, tile, d), lambda b, t: (b, t, 0)),
            const(1, d),
            const(d, 2 * width),
            const(CONV_WIDTH, width),
            const(1, width),
            const(n_blocks, width // n_blocks, 2 * (width // n_blocks)),
            const(n_blocks, 1, 2 * (width // n_blocks)),
            const(1, width),
            const(width, d),
        ],
        out_specs=pl.BlockSpec((None, tile, d), lambda b, t: (b, t, 0)),
        out_shape=jax.ShapeDtypeStruct((bsz, s, d), F32),
        scratch_shapes=[
            pltpu.VMEM((tile + SUBLANES, width), F32),
            pltpu.VMEM((tile, width), F32),
            pltpu.VMEM((tile, width), F32),
            pltpu.VMEM((tile, width), F32),
            pltpu.VMEM((SUBLANES, width), F32),
        ],
        compiler_params=pltpu.CompilerParams(
            dimension_semantics=("arbitrary", "arbitrary"), vmem_limit_bytes=VMEM_LIMIT_BYTES),
        name="lru_layer",
    )(x, g, w_in, conv_w, conv_b, wg, bg, a_param, w_out)


def _fox_in_kernel(x_ref, g_ref, wqt_ref, wk_ref, wvt_ref, wgt_ref, wft_ref, bft_ref,
                   sel_ref, ones_ref, ka_ref, qt_ref, vt_ref, sgt_ref, ct_ref, cart_ref,
                   *, tile, heads, head_dim, scale):
    t = pl.program_id(1)

    @pl.when(t == 0)
    def _():
        cart_ref[...] = jnp.zeros_like(cart_ref)

    hn = _rmsnorm(x_ref[...], g_ref[...]).astype(BF16)

    qt_ref[...] = (_dot_nt(wqt_ref[...], hn) * scale).astype(BF16)
    vt_ref[...] = _dot_nt(wvt_ref[...], hn).astype(BF16)
    gt = _dot_nt(wgt_ref[...], hn)
    sgt_ref[...] = (gt * _sigmoid(gt)).astype(BF16)

    row = lax.broadcasted_iota(jnp.int32, (tile, tile), 0)
    col = lax.broadcasted_iota(jnp.int32, (tile, tile), 1)
    tri_upper = (row <= col).astype(BF16)

    lft = _log_sigmoid(_dot_nt(wft_ref[...], hn) + bft_ref[...][:, 0:1])
    hrow = lax.broadcasted_iota(jnp.int32, lft.shape, 0)
    lft = jnp.where(hrow < heads, lft, 0.0)
    parts = _dot(jnp.concatenate(_split3(lft), axis=0), tri_upper)
    ct = (parts[0:LANES] + parts[LANES:2 * LANES] + parts[2 * LANES:3 * LANES]) + cart_ref[...][:, 0:1]
    cart_ref[...] = jnp.broadcast_to(ct[:, tile - 1:tile], cart_ref.shape)
    ct_ref[...] = ct[0:heads, :]

    c1, c2, c3 = _split3(jnp.transpose(ct) * LOG2E)
    aug = _dot(c1, sel_ref[0]) + _dot(c2, sel_ref[1]) + _dot(c3, sel_ref[2]) + ones_ref[...]

    k = _dot(hn, wk_ref[...])
    lane = lax.broadcasted_iota(jnp.int32, (tile, LANES), 1)
    is_key = lane < head_dim
    for p in range(heads // 2):
        slab = k[:, p * LANES:(p + 1) * LANES]
        ka_ref[2 * p] = jnp.where(is_key, slab, aug).astype(BF16)
        ka_ref[2 * p + 1] = jnp.where(is_key, pltpu.roll(slab, head_dim, 1), aug).astype(BF16)


def _fox_in(x, g, wqt, wk, wvt, wgt, wft, bft, sel, ones, heads, head_dim):
    bsz, s, d = x.shape
    fw = heads * head_dim
    tile = FOX_TILE
    kern = functools.partial(_fox_in_kernel, tile=tile, heads=heads, head_dim=head_dim,
                             scale=LOG2E / (head_dim ** 0.5))
    const = lambda *shape: pl.BlockSpec(shape, lambda b, t: (0,) * len(shape))
    feat_major = pl.BlockSpec((None, fw, tile), lambda b, t: (b, 0, t))
    return pl.pallas_call(
        kern,
        grid=(bsz, s // tile),
        in_specs=[
            pl.BlockSpec((None, tile, d), lambda b, t: (b, t, 0)),
            const(1, d),
            const(fw, d), const(d, fw), const(fw, d), const(fw, d),
            const(LANES, d), const(LANES, LANES),
            const(3, LANES, LANES), const(1, LANES),
        ],
        out_specs=[
            pl.BlockSpec((None, heads, tile, LANES), lambda b, t: (b, 0, t, 0)),
            feat_major, feat_major, feat_major,
            pl.BlockSpec((None, heads, tile), lambda b, t: (b, 0, t)),
        ],
        out_shape=[
            jax.ShapeDtypeStruct((bsz, heads, s, LANES), BF16),
            jax.ShapeDtypeStruct((bsz, fw, s), BF16),
            jax.ShapeDtypeStruct((bsz, fw, s), BF16),
            jax.ShapeDtypeStruct((bsz, fw, s), BF16),
            jax.ShapeDtypeStruct((bsz, heads, s), F32),
        ],
        scratch_shapes=[
            pltpu.VMEM((LANES, LANES), F32),
        ],
        compiler_params=pltpu.CompilerParams(
            dimension_semantics=("arbitrary", "arbitrary"), vmem_limit_bytes=VMEM_LIMIT_BYTES),
        name="fox_in",
    )(x, g, wqt, wk, wvt, wgt, wft, bft, sel, ones)


def _attn_kernel(qt_ref, ct_ref, ka_ref, vt_ref, sgt_ref, yt_ref, qa_ref, vx_ref, s_ref, acc_ref,
                 *, head_dim, **static):
    qi = pl.program_id(2)
    tk, n_kv = static["tk"], static["n_kv"]

    @pl.when(qi == 0)
    def _():
        for hh in range(ATTN_HEADS):
            for j in range(n_kv):
                vx_ref[hh, j, 0:head_dim, :] = vt_ref[hh * head_dim:(hh + 1) * head_dim, j * tk:(j + 1) * tk]
                vx_ref[hh, j, head_dim:, :] = jnp.ones((vx_ref.shape[2] - head_dim, tk), BF16)

    for hh in range(ATTN_HEADS):
        rows = slice(hh * head_dim, (hh + 1) * head_dim)
        _attn_head(pl.program_id(1) * ATTN_HEADS + hh, qi, qt_ref.at[rows], ct_ref, ka_ref.at[hh],
                   sgt_ref.at[rows], yt_ref.at[rows], qa_ref.at[hh], vx_ref.at[hh],
                   s_ref.at[hh, 0], s_ref.at[hh, 1], acc_ref.at[hh], head_dim=head_dim, **static)


def _attn_head(h, qi, qt_ref, ct_ref, ka_ref, sgt_ref, yt_ref, qa_ref, vx_ref, s0_ref, s1_ref,
               acc_ref, *, tq, tk, head_dim, n_kv):
    n_sub = tq // tk
    assert n_sub * tk == tq and n_sub % 2 == 0

    qa_ref[0:head_dim, :] = qt_ref[...]
    c1, c2, c3 = _split3(ct_ref[pl.ds(h, 1), :] * LOG2E)
    arow = lax.broadcasted_iota(jnp.int32, (LANES - head_dim, tq), 0)
    aug = jnp.where((arow >= N_SPLIT * h) & (arow < N_SPLIT * (h + 1)), 1.0, 0.0)
    cbase = C_LANE - head_dim
    aug = jnp.where(arow == cbase, c1.astype(F32), aug)
    aug = jnp.where(arow == cbase + 1, c2.astype(F32), aug)
    aug = jnp.where(arow == cbase + 2, c3.astype(F32), aug)
    qa_ref[head_dim:, :] = aug.astype(BF16)

    def scores(j, qlo=0):
        start = pl.multiple_of(j * tk, tk)
        return _dot(ka_ref[pl.ds(start, tk), :], qa_ref[:, qlo:])

    def update(s_ref, j, m_all, qlo=None):
        lo = qlo or 0
        m_parts = [m_all[:, :lo]] if lo else []
        for c0 in range(lo, tq, ATTN_COLS):
            c1 = min(c0 + ATTN_COLS, tq)
            cs = c0 - lo
            s = s_ref[:, cs:c1 - lo]
            if qlo is not None and cs < tk:
                kr = lax.broadcasted_iota(jnp.int32, s.shape, 0)
                qc = lax.broadcasted_iota(jnp.int32, s.shape, 1) + cs
                s = jnp.where(kr <= qc, s, NEG_INF)
            m = m_all[:, c0:c1]
            m_new = jnp.maximum(m, jnp.max(s, axis=0, keepdims=True))
            alpha = jnp.exp2(m - m_new)
            p = jnp.exp2(s - m_new).astype(BF16)
            acc_ref[:, c0:c1] = acc_ref[:, c0:c1] * alpha + _dot(vx_ref[j], p)
            m_parts.append(m_new)
        return jnp.concatenate(m_parts, axis=1)

    s_refs = (s0_ref, s1_ref)

    def group(i, carry, diagonal):
        for u in range(n_sub):
            j = n_sub * i + u
            if not (diagonal and u == n_sub - 1):
                qlo_next = (u + 1) * tk if diagonal else 0
                s_refs[(u + 1) % 2][:, 0:tq - qlo_next] = scores(j + 1, qlo_next)
            qlo = u * tk if diagonal else 0
            carry = update(s_refs[u % 2], j, carry, qlo=qlo if diagonal else None)
        return carry

    m0 = jnp.full((1, tq), NEG_INF, F32)
    acc_ref[...] = jnp.zeros_like(acc_ref)
    s0_ref[...] = scores(0)
    m = lax.fori_loop(0, qi, lambda i, c: group(i, c, False), m0)
    group(qi, m, True)

    o = acc_ref[0:head_dim, :] / acc_ref[head_dim:head_dim + 1, :]
    yt_ref[...] = (o * sgt_ref[...].astype(F32)).astype(BF16)


def _fox_attn(qt, ct, ka, vt, sgt, heads, head_dim):
    bsz, fw, s = qt.shape
    tq, tk = ATTN_TQ, ATTN_TK
    n_kv = s // tk
    kern = functools.partial(_attn_kernel, tq=tq, tk=tk, head_dim=head_dim, n_kv=n_kv)
    hps = ATTN_HEADS
    assert heads % hps == 0
    q_block = pl.BlockSpec((None, hps * head_dim, tq), lambda b, h, q: (b, h, q))
    return pl.pallas_call(
        kern,
        grid=(bsz, heads // hps, s // tq),
        in_specs=[
            q_block,
            pl.BlockSpec((None, heads, tq), lambda b, h, q: (b, 0, q)),
            pl.BlockSpec((None, hps, s, LANES), lambda b, h, q: (b, h, 0, 0)),
            pl.BlockSpec((None, hps * head_dim, s), lambda b, h, q: (b, h, 0)),
            q_block,
        ],
        out_specs=q_block,
        out_shape=jax.ShapeDtypeStruct((bsz, fw, s), BF16),
        scratch_shapes=[
            pltpu.VMEM((hps, LANES, tq), BF16),
            pltpu.VMEM((hps, n_kv, head_dim + BF16_ROWS, tk), BF16),
            pltpu.VMEM((hps, 2, tk, tq), F32),
            pltpu.VMEM((hps, head_dim + BF16_ROWS, tq), F32),
        ],
        compiler_params=pltpu.CompilerParams(
            dimension_semantics=("arbitrary", "arbitrary", "arbitrary"),
            vmem_limit_bytes=VMEM_LIMIT_BYTES),
        name="fox_attn",
    )(qt, ct, ka, vt, sgt)


def _fox_out_kernel(yt_ref, x_ref, wo_ref, g_ref, o_ref):
    z = x_ref[...] + lax.dot_general(yt_ref[...], wo_ref[...], (((0,), (0,)), ((), ())),
                                     preferred_element_type=F32)
    o_ref[...] = _rmsnorm(z, g_ref[...])


def _fox_out(yt, x, wo, g):
    bsz, s, d = x.shape
    fw = yt.shape[1]
    tile = OUT_TILE
    return pl.pallas_call(
        _fox_out_kernel,
        grid=(bsz, s // tile),
        in_specs=[
            pl.BlockSpec((None, fw, tile), lambda b, t: (b, 0, t)),
            pl.BlockSpec((None, tile, d), lambda b, t: (b, t, 0)),
            pl.BlockSpec((fw, d), lambda b, t: (0, 0)),
            pl.BlockSpec((1, d), lambda b, t: (0, 0)),
        ],
        out_specs=pl.BlockSpec((None, tile, d), lambda b, t: (b, t, 0)),
        out_shape=jax.ShapeDtypeStruct((bsz, s, d), F32),
        compiler_params=pltpu.CompilerParams(
            dimension_semantics=("arbitrary", "arbitrary"), vmem_limit_bytes=VMEM_LIMIT_BYTES),
        name="fox_out",
    )(yt, x, wo, g)


def _key_aug_constants(heads, head_dim):
    h = jnp.arange(LANES)[:, None]
    lane = jnp.arange(LANES)[None, :]
    sel = jnp.stack([jnp.where((h < heads) & (lane == head_dim + N_SPLIT * h + k), -1.0, 0.0)
                     for k in range(N_SPLIT)])
    ones = jnp.where((lane >= C_LANE) & (lane < C_LANE + N_SPLIT), 1.0, 0.0)
    return sel.astype(BF16), ones.astype(F32)


def kernel(x, norm_g, final_g, lru_w_in, lru_conv_w, lru_conv_b, lru_wa, lru_ba, lru_wx, lru_bx,
           lru_a_param, lru_w_out, fox_w_in, fox_b_f, fox_w_out):
    assert norm_g.shape[0] == 2 and lru_w_in.shape[0] == 1 and fox_w_in.shape[0] == 1
    d = x.shape[-1]
    width = lru_w_out.shape[1]
    n_blocks, blk = lru_wa.shape[1], lru_wa.shape[2]
    heads = fox_b_f.shape[1]
    fw = fox_w_out.shape[1]
    head_dim = fw // heads
    assert 2 * head_dim == LANES and heads % 2 == 0
    assert head_dim + N_SPLIT * heads <= C_LANE and C_LANE + N_SPLIT <= LANES

    wg = jnp.concatenate([lru_wa[0], lru_wx[0]], axis=-1).astype(BF16)
    bg = jnp.concatenate([lru_ba[0].reshape(n_blocks, 1, blk), lru_bx[0].reshape(n_blocks, 1, blk)], axis=-1)
    x1 = _lru_layer(x, norm_g[0:1], lru_w_in[0].astype(BF16), lru_conv_w[0], lru_conv_b[0:1], wg, bg,
                    lru_a_param[0:1], lru_w_out[0].astype(BF16), n_blocks)

    w = fox_w_in[0]
    wqt = w[:, 0 * fw:1 * fw].T.astype(BF16)
    wk = w[:, 1 * fw:2 * fw].astype(BF16)
    wvt = w[:, 2 * fw:3 * fw].T.astype(BF16)
    wgt = w[:, 3 * fw:4 * fw].T.astype(BF16)
    wft = jnp.pad(w[:, 4 * fw:].T, ((0, LANES - heads), (0, 0))).astype(BF16)
    bft = jnp.broadcast_to(jnp.pad(fox_b_f[0], (0, LANES - heads))[:, None], (LANES, LANES))
    sel, ones = _key_aug_constants(heads, head_dim)
    ka, qt, vt, sgt, ct = _fox_in(x1, norm_g[1:2], wqt, wk, wvt, wgt, wft, bft, sel, ones,
                                  heads, head_dim)
    yt = _fox_attn(qt, ct, ka, vt, sgt, heads, head_dim)
    return _fox_out(yt, x1, fox_w_out[0].astype(BF16), final_g.reshape(1, d))
```

```python
import functools

import jax
import jax.numpy as jnp
from jax import lax
from jax.experimental import pallas as pl
from jax.experimental.pallas import tpu as pltpu

F32 = jnp.float32
BF16 = jnp.bfloat16

EPS = 1e-6
LRU_C = 8.0
CONV_WIDTH = 4
NEG_INF = -1e30
LOG2E = 1.4426950408889634

LANES = 128
SUBLANES = 8
BF16_ROWS = 16
VMEM_LIMIT_BYTES = 56 * 1024 * 1024

LRU_TILE = 512
FOX_TILE = 512
OUT_TILE = 1024
ATTN_TQ = 2048
ATTN_TK = 256
ATTN_HEADS = 2
ATTN_COLS = 512

N_SPLIT = 3
C_LANE = 112


def _rmsnorm(x, g):
    return x * lax.rsqrt(jnp.mean(x * x, axis=-1, keepdims=True) + EPS) * g


def _silu(x):
    h = 0.5 * x
    return h + h * jnp.tanh(h)


def _log_sigmoid(x):
    return jnp.minimum(x, 0.0) - jnp.log1p(jnp.exp(-jnp.abs(x)))


def _split3(c):
    c1 = c.astype(BF16)
    r1 = c - c1.astype(F32)
    c2 = r1.astype(BF16)
    c3 = (r1 - c2.astype(F32)).astype(BF16)
    return c1, c2, c3


def _dot(a, b):
    return jnp.dot(a, b, preferred_element_type=F32)


def _dot_nt(a, b):
    return lax.dot_general(a, b, (((1,), (1,)), ((), ())), preferred_element_type=F32)


def _lru_kernel(x_ref, g_ref, win_ref, cw_ref, cb_ref, wg_ref, bg_ref, ap_ref, wout_ref, o_ref,
                ext_ref, a_ref, b_ref, hs_ref, hcar_ref, *, tile, width, n_blocks):
    t = pl.program_id(1)
    blk = width // n_blocks

    @pl.when(t == 0)
    def _():
        ext_ref[0:SUBLANES, :] = jnp.zeros((SUBLANES, width), F32)
        hcar_ref[...] = jnp.zeros_like(hcar_ref)

    x = x_ref[...]
    xn = _rmsnorm(x, g_ref[...]).astype(BF16)
    u = _dot(xn, win_ref[...])
    gate = u[:, width:]
    ext_ref[SUBLANES:SUBLANES + tile, :] = u[:, :width]

    cw = cw_ref[...]
    xc = cb_ref[...] + cw[3:4] * ext_ref[SUBLANES:SUBLANES + tile, :]
    for k in range(CONV_WIDTH - 1):
        off = SUBLANES - (CONV_WIDTH - 1) + k
        xc = xc + cw[k:k + 1] * ext_ref[off:off + tile, :]
    ext_ref[0:SUBLANES, :] = ext_ref[tile:tile + SUBLANES, :]

    ap = ap_ref[...]
    sp = jnp.maximum(-ap, 0.0) + jnp.log1p(jnp.exp(-jnp.abs(ap)))
    xcb = xc.astype(BF16)
    half_c_sp = (0.5 * LRU_C) * sp
    for n in range(n_blocks):
        sl = slice(n * blk, (n + 1) * blk)
        g = _dot(xcb[:, sl], wg_ref[n]) + bg_ref[n]
        tr = jnp.tanh(g[:, :blk])
        i = 0.5 + 0.5 * jnp.tanh(g[:, blk:])
        neg_log_a = half_c_sp[:, sl] + half_c_sp[:, sl] * tr
        a = jnp.exp(-neg_log_a)
        a_ref[:, sl] = a
        one_minus_a2 = jnp.tanh(neg_log_a) * (1.0 + a * a)
        mult = jnp.where(one_minus_a2 > 0.0, one_minus_a2 * lax.rsqrt(one_minus_a2), 0.0)
        b_ref[:, sl] = mult * (i * xc[:, sl])


    def step(i, h):
        h = a_ref[pl.ds(i, 1), :] * h + b_ref[pl.ds(i, 1), :]
        hs_ref[pl.ds(i, 1), :] = h
        return h

    h = lax.fori_loop(0, tile, step, hcar_ref[0:1, :], unroll=SUBLANES)
    hcar_ref[0:1, :] = h

    y = hs_ref[...] * _silu(gate)
    o_ref[...] = x + _dot(y.astype(BF16), wout_ref[...])


def _lru_layer(x, g, w_in, conv_w, conv_b, wg, bg, a_param, w_out, n_blocks):
    bsz, s, d = x.shape
    width = w_out.shape[0]
    tile = LRU_TILE
    kern = functools.partial(_lru_kernel, tile=tile, width=width, n_blocks=n_blocks)
    const = lambda *shape: pl.BlockSpec(shape, lambda b, t: (0,) * len(shape))
    return pl.pallas_call(
        kern,
        grid=(bsz, s // tile),
        in_specs=[
            pl.BlockSpec((None, tile, d), lambda b, t: (b, t, 0)),
            const(1, d),
            const(d, 2 * width),
            const(CONV_WIDTH, width),
            const(1, width),
            const(n_blocks, width // n_blocks, 2 * (width // n_blocks)),
            const(n_blocks, 1, 2 * (width // n_blocks)),
            const(1, width),
            const(width, d),
        ],
        out_specs=pl.BlockSpec((None, tile, d), lambda b, t: (b, t, 0)),
        out_shape=jax.ShapeDtypeStruct((bsz, s, d), F32),
        scratch_shapes=[
            pltpu.VMEM((tile + SUBLANES, width), F32),
            pltpu.VMEM((tile, width), F32),
            pltpu.VMEM((tile, width), F32),
            pltpu.VMEM((tile, width), F32),
            pltpu.VMEM((SUBLANES, width), F32),
        ],
        compiler_params=pltpu.CompilerParams(
            dimension_semantics=("arbitrary", "arbitrary"), vmem_limit_bytes=VMEM_LIMIT_BYTES),
        name="lru_layer",
    )(x, g, w_in, conv_w, conv_b, wg, bg, a_param, w_out)


def _fox_in_kernel(x_ref, g_ref, wqt_ref, wk_ref, wvt_ref, wgt_ref, wft_ref, bft_ref,
                   sel_ref, ones_ref, ka_ref, qt_ref, vt_ref, sgt_ref, ct_ref, cart_ref,
                   *, tile, heads, head_dim, scale):
    t = pl.program_id(1)

    @pl.when(t == 0)
    def _():
        cart_ref[...] = jnp.zeros_like(cart_ref)

    hn = _rmsnorm(x_ref[...], g_ref[...]).astype(BF16)

    qt_ref[...] = (_dot_nt(wqt_ref[...], hn) * scale).astype(BF16)
    vt_ref[...] = _dot_nt(wvt_ref[...], hn).astype(BF16)
    gt = _dot_nt(wgt_ref[...], hn)
    sgt_ref[...] = _silu(gt).astype(BF16)

    row = lax.broadcasted_iota(jnp.int32, (tile, tile), 0)
    col = lax.broadcasted_iota(jnp.int32, (tile, tile), 1)
    tri_upper = (row <= col).astype(BF16)

    lft = _log_sigmoid(_dot_nt(wft_ref[...], hn) + bft_ref[...][:, 0:1])
    hrow = lax.broadcasted_iota(jnp.int32, lft.shape, 0)
    lft = jnp.where(hrow < heads, lft, 0.0)
    parts = _dot(jnp.concatenate(_split3(lft), axis=0), tri_upper)
    ct = (parts[0:LANES] + parts[LANES:2 * LANES] + parts[2 * LANES:3 * LANES]) + cart_ref[...][:, 0:1]
    cart_ref[...] = jnp.broadcast_to(ct[:, tile - 1:tile], cart_ref.shape)
    ct_ref[...] = ct[0:heads, :]

    c1, c2, c3 = _split3(jnp.transpose(ct) * LOG2E)
    aug = _dot(c1, sel_ref[0]) + _dot(c2, sel_ref[1]) + _dot(c3, sel_ref[2]) + ones_ref[...]

    k = _dot(hn, wk_ref[...])
    lane = lax.broadcasted_iota(jnp.int32, (tile, LANES), 1)
    is_key = lane < head_dim
    for p in range(heads // 2):
        slab = k[:, p * LANES:(p + 1) * LANES]
        ka_ref[2 * p] = jnp.where(is_key, slab, aug).astype(BF16)
        ka_ref[2 * p + 1] = jnp.where(is_key, pltpu.roll(slab, head_dim, 1), aug).astype(BF16)


def _fox_in(x, g, wqt, wk, wvt, wgt, wft, bft, sel, ones, heads, head_dim):
    bsz, s, d = x.shape
    fw = heads * head_dim
    tile = FOX_TILE
    kern = functools.partial(_fox_in_kernel, tile=tile, heads=heads, head_dim=head_dim,
                             scale=LOG2E / (head_dim ** 0.5))
    const = lambda *shape: pl.BlockSpec(shape, lambda b, t: (0,) * len(shape))
    feat_major = pl.BlockSpec((None, fw, tile), lambda b, t: (b, 0, t))
    return pl.pallas_call(
        kern,
        grid=(bsz, s // tile),
        in_specs=[
            pl.BlockSpec((None, tile, d), lambda b, t: (b, t, 0)),
            const(1, d),
            const(fw, d), const(d, fw), const(fw, d), const(fw, d),
            const(LANES, d), const(LANES, LANES),
            const(3, LANES, LANES), const(1, LANES),
        ],
        out_specs=[
            pl.BlockSpec((None, heads, tile, LANES), lambda b, t: (b, 0, t, 0)),
            feat_major, feat_major, feat_major,
            pl.BlockSpec((None, heads, tile), lambda b, t: (b, 0, t)),
        ],
        out_shape=[
            jax.ShapeDtypeStruct((bsz, heads, s, LANES), BF16),
            jax.ShapeDtypeStruct((bsz, fw, s), BF16),
            jax.ShapeDtypeStruct((bsz, fw, s), BF16),
            jax.ShapeDtypeStruct((bsz, fw, s), BF16),
            jax.ShapeDtypeStruct((bsz, heads, s), F32),
        ],
        scratch_shapes=[
            pltpu.VMEM((LANES, LANES), F32),
        ],
        compiler_params=pltpu.CompilerParams(
            dimension_semantics=("arbitrary", "arbitrary"), vmem_limit_bytes=VMEM_LIMIT_BYTES),
        name="fox_in",
    )(x, g, wqt, wk, wvt, wgt, wft, bft, sel, ones)


def _attn_kernel(qt_ref, ct_ref, ka_ref, vt_ref, sgt_ref, yt_ref, qa_ref, vx_ref, s_ref, acc_ref,
                 *, head_dim, **static):
    qi = pl.program_id(2)
    tk, n_kv = static["tk"], static["n_kv"]

    @pl.when(qi == 0)
    def _():
        for hh in range(ATTN_HEADS):
            for j in range(n_kv):
                vx_ref[hh, j, 0:head_dim, :] = vt_ref[hh * head_dim:(hh + 1) * head_dim, j * tk:(j + 1) * tk]
                vx_ref[hh, j, head_dim:, :] = jnp.ones((vx_ref.shape[2] - head_dim, tk), BF16)

    for hh in range(ATTN_HEADS):
        rows = slice(hh * head_dim, (hh + 1) * head_dim)
        _attn_head(pl.program_id(1) * ATTN_HEADS + hh, qi, qt_ref.at[rows], ct_ref, ka_ref.at[hh],
                   sgt_ref.at[rows], yt_ref.at[rows], qa_ref.at[hh], vx_ref.at[hh],
                   s_ref.at[hh, 0], s_ref.at[hh, 1], acc_ref.at[hh], head_dim=head_dim, **static)


def _attn_head(h, qi, qt_ref, ct_ref, ka_ref, sgt_ref, yt_ref, qa_ref, vx_ref, s0_ref, s1_ref,
               acc_ref, *, tq, tk, head_dim, n_kv):
    n_sub = tq // tk
    assert n_sub * tk == tq and n_sub % 2 == 0

    qa_ref[0:head_dim, :] = qt_ref[...]
    c1, c2, c3 = _split3(ct_ref[pl.ds(h, 1), :] * LOG2E)
    arow = lax.broadcasted_iota(jnp.int32, (LANES - head_dim, tq), 0)
    aug = jnp.where((arow >= N_SPLIT * h) & (arow < N_SPLIT * (h + 1)), 1.0, 0.0)
    cbase = C_LANE - head_dim
    aug = jnp.where(arow == cbase, c1.astype(F32), aug)
    aug = jnp.where(arow == cbase + 1, c2.astype(F32), aug)
    aug = jnp.where(arow == cbase + 2, c3.astype(F32), aug)
    qa_ref[head_dim:, :] = aug.astype(BF16)

    def scores(j, qlo=0):
        start = pl.multiple_of(j * tk, tk)
        return _dot(ka_ref[pl.ds(start, tk), :], qa_ref[:, qlo:])

    def update(s_ref, j, m_all, qlo=None):
        lo = qlo or 0
        m_parts = [m_all[:, :lo]] if lo else []
        for c0 in range(lo, tq, ATTN_COLS):
            c1 = min(c0 + ATTN_COLS, tq)
            cs = c0 - lo
            s = s_ref[:, cs:c1 - lo]
            if qlo is not None and cs < tk:
                kr = lax.broadcasted_iota(jnp.int32, s.shape, 0)
                qc = lax.broadcasted_iota(jnp.int32, s.shape, 1) + cs
                s = jnp.where(kr <= qc, s, NEG_INF)
            m = m_all[:, c0:c1]
            m_new = jnp.maximum(m, jnp.max(s, axis=0, keepdims=True))
            alpha = jnp.exp2(m - m_new)
            p = jnp.exp2(s - m_new).astype(BF16)
            acc_ref[:, c0:c1] = acc_ref[:, c0:c1] * alpha + _dot(vx_ref[j], p)
            m_parts.append(m_new)
        return jnp.concatenate(m_parts, axis=1)

    s_refs = (s0_ref, s1_ref)

    def group(i, carry, diagonal):
        for u in range(n_sub):
            j = n_sub * i + u
            if not (diagonal and u == n_sub - 1):
                qlo_next = (u + 1) * tk if diagonal else 0
                s_refs[(u + 1) % 2][:, 0:tq - qlo_next] = scores(j + 1, qlo_next)
            qlo = u * tk if diagonal else 0
            carry = update(s_refs[u % 2], j, carry, qlo=qlo if diagonal else None)
        return carry

    m0 = jnp.full((1, tq), NEG_INF, F32)
    acc_ref[...] = jnp.zeros_like(acc_ref)
    s0_ref[...] = scores(0)
    m = lax.fori_loop(0, qi, lambda i, c: group(i, c, False), m0)
    group(qi, m, True)

    o = acc_ref[0:head_dim, :] / acc_ref[head_dim:head_dim + 1, :]
    yt_ref[...] = (o * sgt_ref[...].astype(F32)).astype(BF16)


def _fox_attn(qt, ct, ka, vt, sgt, heads, head_dim):
    bsz, fw, s = qt.shape
    tq, tk = ATTN_TQ, ATTN_TK
    n_kv = s // tk
    kern = functools.partial(_attn_kernel, tq=tq, tk=tk, head_dim=head_dim, n_kv=n_kv)
    hps = ATTN_HEADS
    assert heads % hps == 0
    q_block = pl.BlockSpec((None, hps * head_dim, tq), lambda b, h, q: (b, h, q))
    return pl.pallas_call(
        kern,
        grid=(bsz, heads // hps, s // tq),
        in_specs=[
            q_block,
            pl.BlockSpec((None, heads, tq), lambda b, h, q: (b, 0, q)),
            pl.BlockSpec((None, hps, s, LANES), lambda b, h, q: (b, h, 0, 0)),
            pl.BlockSpec((None, hps * head_dim, s), lambda b, h, q: (b, h, 0)),
            q_block,
        ],
        out_specs=q_block,
        out_shape=jax.ShapeDtypeStruct((bsz, fw, s), BF16),
        scratch_shapes=[
            pltpu.VMEM((hps, LANES, tq), BF16),
            pltpu.VMEM((hps, n_kv, head_dim + BF16_ROWS, tk), BF16),
            pltpu.VMEM((hps, 2, tk, tq), F32),
            pltpu.VMEM((hps, head_dim + BF16_ROWS, tq), F32),
        ],
        compiler_params=pltpu.CompilerParams(
            dimension_semantics=("arbitrary", "arbitrary", "arbitrary"),
            vmem_limit_bytes=VMEM_LIMIT_BYTES),
        name="fox_attn",
    )(qt, ct, ka, vt, sgt)


def _fox_out_kernel(yt_ref, x_ref, wo_ref, g_ref, o_ref):
    z = x_ref[...] + lax.dot_general(yt_ref[...], wo_ref[...], (((0,), (0,)), ((), ())),
                                     preferred_element_type=F32)
    o_ref[...] = _rmsnorm(z, g_ref[...])


def _fox_out(yt, x, wo, g):
    bsz, s, d = x.shape
    fw = yt.shape[1]
    tile = OUT_TILE
    return pl.pallas_call(
        _fox_out_kernel,
        grid=(bsz, s // tile),
        in_specs=[
            pl.BlockSpec((None, fw, tile), lambda b, t: (b, 0, t)),
            pl.BlockSpec((None, tile, d), lambda b, t: (b, t, 0)),
            pl.BlockSpec((fw, d), lambda b, t: (0, 0)),
            pl.BlockSpec((1, d), lambda b, t: (0, 0)),
        ],
        out_specs=pl.BlockSpec((None, tile, d), lambda b, t: (b, t, 0)),
        out_shape=jax.ShapeDtypeStruct((bsz, s, d), F32),
        compiler_params=pltpu.CompilerParams(
            dimension_semantics=("arbitrary", "arbitrary"), vmem_limit_bytes=VMEM_LIMIT_BYTES),
        name="fox_out",
    )(yt, x, wo, g)


def _key_aug_constants(heads, head_dim):
    h = jnp.arange(LANES)[:, None]
    lane = jnp.arange(LANES)[None, :]
    sel = jnp.stack([jnp.where((h < heads) & (lane == head_dim + N_SPLIT * h + k), -1.0, 0.0)
                     for k in range(N_SPLIT)])
    ones = jnp.where((lane >= C_LANE) & (lane < C_LANE + N_SPLIT), 1.0, 0.0)
    return sel.astype(BF16), ones.astype(F32)


def kernel(x, norm_g, final_g, lru_w_in, lru_conv_w, lru_conv_b, lru_wa, lru_ba, lru_wx, lru_bx,
           lru_a_param, lru_w_out, fox_w_in, fox_b_f, fox_w_out):
    assert norm_g.shape[0] == 2 and lru_w_in.shape[0] == 1 and fox_w_in.shape[0] == 1
    d = x.shape[-1]
    width = lru_w_out.shape[1]
    n_blocks, blk = lru_wa.shape[1], lru_wa.shape[2]
    heads = fox_b_f.shape[1]
    fw = fox_w_out.shape[1]
    head_dim = fw // heads
    assert 2 * head_dim == LANES and heads % 2 == 0
    assert head_dim + N_SPLIT * heads <= C_LANE and C_LANE + N_SPLIT <= LANES

    wg = (0.5 * jnp.concatenate([lru_wa[0], lru_wx[0]], axis=-1)).astype(BF16)
    bg = 0.5 * jnp.concatenate([lru_ba[0].reshape(n_blocks, 1, blk), lru_bx[0].reshape(n_blocks, 1, blk)],
                               axis=-1)
    x1 = _lru_layer(x, norm_g[0:1], lru_w_in[0].astype(BF16), lru_conv_w[0], lru_conv_b[0:1], wg, bg,
                    lru_a_param[0:1], lru_w_out[0].astype(BF16), n_blocks)

    w = fox_w_in[0]
    wqt = w[:, 0 * fw:1 * fw].T.astype(BF16)
    wk = w[:, 1 * fw:2 * fw].astype(BF16)
    wvt = w[:, 2 * fw:3 * fw].T.astype(BF16)
    wgt = w[:, 3 * fw:4 * fw].T.astype(BF16)
    wft = jnp.pad(w[:, 4 * fw:].T, ((0, LANES - heads), (0, 0))).astype(BF16)
    bft = jnp.broadcast_to(jnp.pad(fox_b_f[0], (0, LANES - heads))[:, None], (LANES, LANES))
    sel, ones = _key_aug_constants(heads, head_dim)
    ka, qt, vt, sgt, ct = _fox_in(x1, norm_g[1:2], wqt, wk, wvt, wgt, wft, bft, sel, ones,
                                  heads, head_dim)
    yt = _fox_attn(qt, ct, ka, vt, sgt, heads, head_dim)
    return _fox_out(yt, x1, fox_w_out[0].astype(BF16), final_g.reshape(1, d))
```

```python
import functools

import jax
import jax.numpy as jnp
from jax import lax
from jax.experimental import pallas as pl
from jax.experimental.pallas import tpu as pltpu

F32 = jnp.float32
BF16 = jnp.bfloat16

EPS = 1e-6
LRU_C = 8.0
CONV_WIDTH = 4
NEG_INF = -1e30
LOG2E = 1.4426950408889634

LANES = 128
SUBLANES = 8
BF16_ROWS = 16
VMEM_LIMIT_BYTES = 56 * 1024 * 1024

LRU_TILE = 512
FOX_TILE = 512
OUT_TILE = 1024
ATTN_TQ = 2048
ATTN_TK = 256
ATTN_HEADS = 2
ATTN_COLS = 512

N_SPLIT = 3
C_LANE = 112


def _rmsnorm(x, g):
    return x * lax.rsqrt(jnp.mean(x * x, axis=-1, keepdims=True) + EPS) * g


def _silu(x):
    h = 0.5 * x
    return h + h * jnp.tanh(h)


def _log_sigmoid(x):
    return jnp.minimum(x, 0.0) - jnp.log1p(jnp.exp(-jnp.abs(x)))


def _split3(c):
    c1 = c.astype(BF16)
    r1 = c - c1.astype(F32)
    c2 = r1.astype(BF16)
    c3 = (r1 - c2.astype(F32)).astype(BF16)
    return c1, c2, c3


def _dot(a, b):
    return jnp.dot(a, b, preferred_element_type=F32)


def _dot_nt(a, b):
    return lax.dot_general(a, b, (((1,), (1,)), ((), ())), preferred_element_type=F32)


def _lru_kernel(x_ref, g_ref, win_ref, cw_ref, cb_ref, wg_ref, bg_ref, ap_ref, wout_ref, o_ref,
                ext_ref, a_ref, b_ref, hs_ref, hcar_ref, *, tile, width, n_blocks):
    t = pl.program_id(1)
    blk = width // n_blocks

    @pl.when(t == 0)
    def _():
        ext_ref[0:SUBLANES, :] = jnp.zeros((SUBLANES, width), F32)
        hcar_ref[...] = jnp.zeros_like(hcar_ref)

    x = x_ref[...]
    xn = _rmsnorm(x, g_ref[...]).astype(BF16)
    u = _dot(xn, win_ref[...])
    gate = u[:, width:]
    ext_ref[SUBLANES:SUBLANES + tile, :] = u[:, :width]

    cw = cw_ref[...]
    xc = cb_ref[...] + cw[3:4] * ext_ref[SUBLANES:SUBLANES + tile, :]
    for k in range(CONV_WIDTH - 1):
        off = SUBLANES - (CONV_WIDTH - 1) + k
        xc = xc + cw[k:k + 1] * ext_ref[off:off + tile, :]
    ext_ref[0:SUBLANES, :] = ext_ref[tile:tile + SUBLANES, :]

    ap = ap_ref[...]
    sp = jnp.maximum(-ap, 0.0) + jnp.log1p(jnp.exp(-jnp.abs(ap)))
    xcb = xc.astype(BF16)
    half_c_sp = (0.5 * LRU_C) * sp
    for n in range(n_blocks):
        sl = slice(n * blk, (n + 1) * blk)
        g = _dot(xcb[:, sl], wg_ref[n]) + bg_ref[n]
        tr = jnp.tanh(g[:, :blk])
        i = 0.5 + 0.5 * jnp.tanh(g[:, blk:])
        neg_log_a = half_c_sp[:, sl] + half_c_sp[:, sl] * tr
        a = jnp.exp(-neg_log_a)
        a_ref[:, sl] = a
        one_minus_a2 = jnp.tanh(neg_log_a) * (1.0 + a * a)
        mult = jnp.where(one_minus_a2 > 0.0, one_minus_a2 * lax.rsqrt(one_minus_a2), 0.0)
        b_ref[:, sl] = mult * (i * xc[:, sl])


    def step(i, h):
        h = a_ref[pl.ds(i, 1), :] * h + b_ref[pl.ds(i, 1), :]
        hs_ref[pl.ds(i, 1), :] = h
        return h

    h = lax.fori_loop(0, tile, step, hcar_ref[0:1, :], unroll=SUBLANES)
    hcar_ref[0:1, :] = h

    y = hs_ref[...] * _silu(gate)
    o_ref[...] = x + _dot(y.astype(BF16), wout_ref[...])


def _lru_layer(x, g, w_in, conv_w, conv_b, wg, bg, a_param, w_out, n_blocks):
    bsz, s, d = x.shape
    width = w_out.shape[0]
    tile = LRU_TILE
    kern = functools.partial(_lru_kernel, tile=tile, width=width, n_blocks=n_blocks)
    const = lambda *shape: pl.BlockSpec(shape, lambda b, t: (0,) * len(shape))
    return pl.pallas_call(
        kern,
        grid=(bsz, s // tile),
        in_specs=[
            pl.BlockSpec((None, tile, d), lambda b, t: (b, t, 0)),
            const(1, d),
            const(d, 2 * width),
            const(CONV_WIDTH, width),
            const(1, width),
            const(n_blocks, width // n_blocks, 2 * (width // n_blocks)),
            const(n_blocks, 1, 2 * (width // n_blocks)),
            const(1, width),
            const(width, d),
        ],
        out_specs=pl.BlockSpec((None, tile, d), lambda b, t: (b, t, 0)),
        out_shape=jax.ShapeDtypeStruct((bsz, s, d), F32),
        scratch_shapes=[
            pltpu.VMEM((tile + SUBLANES, width), F32),
            pltpu.VMEM((tile, width), F32),
            pltpu.VMEM((tile, width), F32),
            pltpu.VMEM((tile, width), F32),
            pltpu.VMEM((SUBLANES, width), F32),
        ],
        compiler_params=pltpu.CompilerParams(
            dimension_semantics=("arbitrary", "arbitrary"), vmem_limit_bytes=VMEM_LIMIT_BYTES),
        name="lru_layer",
    )(x, g, w_in, conv_w, conv_b, wg, bg, a_param, w_out)


def _fox_in_kernel(x_ref, g_ref, wqt_ref, wk_ref, wvt_ref, wgt_ref, wft_ref, bft_ref,
                   sel_ref, ones_ref, ka_ref, qt_ref, vt_ref, sgt_ref, ct_ref, cart_ref,
                   *, tile, heads, head_dim, scale):
    t = pl.program_id(1)

    @pl.when(t == 0)
    def _():
        cart_ref[...] = jnp.zeros_like(cart_ref)

    hn = _rmsnorm(x_ref[...], g_ref[...]).astype(BF16)

    qt_ref[...] = (_dot_nt(wqt_ref[...], hn) * scale).astype(BF16)
    vt_ref[...] = _dot_nt(wvt_ref[...], hn).astype(BF16)
    gt = _dot_nt(wgt_ref[...], hn)
    sgt_ref[...] = _silu(gt).astype(BF16)

    row = lax.broadcasted_iota(jnp.int32, (tile, tile), 0)
    col = lax.broadcasted_iota(jnp.int32, (tile, tile), 1)
    tri_upper = (row <= col).astype(BF16)

    lft = _log_sigmoid(_dot_nt(wft_ref[...], hn) + bft_ref[...][:, 0:1])
    hrow = lax.broadcasted_iota(jnp.int32, lft.shape, 0)
    lft = jnp.where(hrow < heads, lft, 0.0)
    parts = _dot(jnp.concatenate(_split3(lft), axis=0), tri_upper)
    ct = (parts[0:LANES] + parts[LANES:2 * LANES] + parts[2 * LANES:3 * LANES]) + cart_ref[...][:, 0:1]
    cart_ref[...] = jnp.broadcast_to(ct[:, tile - 1:tile], cart_ref.shape)
    ct_ref[...] = ct[0:heads, :]

    c_parts = jnp.concatenate(_split3(jnp.transpose(ct) * LOG2E), axis=1)
    aug = _dot(c_parts, sel_ref[...]) + ones_ref[...]

    k = _dot(hn, wk_ref[...])
    lane = lax.broadcasted_iota(jnp.int32, (tile, LANES), 1)
    is_key = lane < head_dim
    for p in range(heads // 2):
        slab = k[:, p * LANES:(p + 1) * LANES]
        ka_ref[2 * p] = jnp.where(is_key, slab, aug).astype(BF16)
        ka_ref[2 * p + 1] = jnp.where(is_key, pltpu.roll(slab, head_dim, 1), aug).astype(BF16)


def _fox_in(x, g, wqt, wk, wvt, wgt, wft, bft, sel, ones, heads, head_dim):
    bsz, s, d = x.shape
    fw = heads * head_dim
    tile = FOX_TILE
    kern = functools.partial(_fox_in_kernel, tile=tile, heads=heads, head_dim=head_dim,
                             scale=LOG2E / (head_dim ** 0.5))
    const = lambda *shape: pl.BlockSpec(shape, lambda b, t: (0,) * len(shape))
    feat_major = pl.BlockSpec((None, fw, tile), lambda b, t: (b, 0, t))
    return pl.pallas_call(
        kern,
        grid=(bsz, s // tile),
        in_specs=[
            pl.BlockSpec((None, tile, d), lambda b, t: (b, t, 0)),
            const(1, d),
            const(fw, d), const(d, fw), const(fw, d), const(fw, d),
            const(LANES, d), const(LANES, LANES),
            const(N_SPLIT * LANES, LANES), const(1, LANES),
        ],
        out_specs=[
            pl.BlockSpec((None, heads, tile, LANES), lambda b, t: (b, 0, t, 0)),
            feat_major, feat_major, feat_major,
            pl.BlockSpec((None, heads, tile), lambda b, t: (b, 0, t)),
        ],
        out_shape=[
            jax.ShapeDtypeStruct((bsz, heads, s, LANES), BF16),
            jax.ShapeDtypeStruct((bsz, fw, s), BF16),
            jax.ShapeDtypeStruct((bsz, fw, s), BF16),
            jax.ShapeDtypeStruct((bsz, fw, s), BF16),
            jax.ShapeDtypeStruct((bsz, heads, s), F32),
        ],
        scratch_shapes=[
            pltpu.VMEM((LANES, LANES), F32),
        ],
        compiler_params=pltpu.CompilerParams(
            dimension_semantics=("arbitrary", "arbitrary"), vmem_limit_bytes=VMEM_LIMIT_BYTES),
        name="fox_in",
    )(x, g, wqt, wk, wvt, wgt, wft, bft, sel, ones)


def _attn_kernel(qt_ref, ct_ref, ka_ref, vt_ref, sgt_ref, yt_ref, qa_ref, vx_ref, s_ref, acc_ref,
                 *, head_dim, **static):
    qi = pl.program_id(2)
    tk, n_kv = static["tk"], static["n_kv"]

    @pl.when(qi == 0)
    def _():
        for hh in range(ATTN_HEADS):
            for j in range(n_kv):
                vx_ref[hh, j, 0:head_dim, :] = vt_ref[hh * head_dim:(hh + 1) * head_dim, j * tk:(j + 1) * tk]
                vx_ref[hh, j, head_dim:, :] = jnp.ones((vx_ref.shape[2] - head_dim, tk), BF16)

    for hh in range(ATTN_HEADS):
        rows = slice(hh * head_dim, (hh + 1) * head_dim)
        _attn_head(pl.program_id(1) * ATTN_HEADS + hh, qi, qt_ref.at[rows], ct_ref, ka_ref.at[hh],
                   sgt_ref.at[rows], yt_ref.at[rows], qa_ref.at[hh], vx_ref.at[hh],
                   s_ref.at[hh, 0], s_ref.at[hh, 1], acc_ref.at[hh], head_dim=head_dim, **static)


def _attn_head(h, qi, qt_ref, ct_ref, ka_ref, sgt_ref, yt_ref, qa_ref, vx_ref, s0_ref, s1_ref,
               acc_ref, *, tq, tk, head_dim, n_kv):
    n_sub = tq // tk
    assert n_sub * tk == tq and n_sub % 2 == 0

    qa_ref[0:head_dim, :] = qt_ref[...]
    c1, c2, c3 = _split3(ct_ref[pl.ds(h, 1), :] * LOG2E)
    arow = lax.broadcasted_iota(jnp.int32, (LANES - head_dim, tq), 0)
    aug = jnp.where((arow >= N_SPLIT * h) & (arow < N_SPLIT * (h + 1)), 1.0, 0.0)
    cbase = C_LANE - head_dim
    aug = jnp.where(arow == cbase, c1.astype(F32), aug)
    aug = jnp.where(arow == cbase + 1, c2.astype(F32), aug)
    aug = jnp.where(arow == cbase + 2, c3.astype(F32), aug)
    qa_ref[head_dim:, :] = aug.astype(BF16)

    def scores(j, qlo=0):
        start = pl.multiple_of(j * tk, tk)
        return _dot(ka_ref[pl.ds(start, tk), :], qa_ref[:, qlo:])

    def update(s_ref, j, m_all, qlo=None):
        lo = qlo or 0
        m_parts = [m_all[:, :lo]] if lo else []
        for c0 in range(lo, tq, ATTN_COLS):
            c1 = min(c0 + ATTN_COLS, tq)
            cs = c0 - lo
            s = s_ref[:, cs:c1 - lo]
            if qlo is not None and cs < tk:
                kr = lax.broadcasted_iota(jnp.int32, s.shape, 0)
                qc = lax.broadcasted_iota(jnp.int32, s.shape, 1) + cs
                s = jnp.where(kr <= qc, s, NEG_INF)
            m = m_all[:, c0:c1]
            m_new = jnp.maximum(m, jnp.max(s, axis=0, keepdims=True))
            alpha = jnp.exp2(m - m_new)
            p = jnp.exp2(s - m_new).astype(BF16)
            acc_ref[:, c0:c1] = acc_ref[:, c0:c1] * alpha + _dot(vx_ref[j], p)
            m_parts.append(m_new)
        return jnp.concatenate(m_parts, axis=1)

    s_refs = (s0_ref, s1_ref)

    def group(i, carry, diagonal):
        for u in range(n_sub):
            j = n_sub * i + u
            if not (diagonal and u == n_sub - 1):
                qlo_next = (u + 1) * tk if diagonal else 0
                s_refs[(u + 1) % 2][:, 0:tq - qlo_next] = scores(j + 1, qlo_next)
            qlo = u * tk if diagonal else 0
            carry = update(s_refs[u % 2], j, carry, qlo=qlo if diagonal else None)
        return carry

    m0 = jnp.full((1, tq), NEG_INF, F32)
    acc_ref[...] = jnp.zeros_like(acc_ref)
    s0_ref[...] = scores(0)
    m = lax.fori_loop(0, qi, lambda i, c: group(i, c, False), m0)
    group(qi, m, True)

    o = acc_ref[0:head_dim, :] / acc_ref[head_dim:head_dim + 1, :]
    yt_ref[...] = (o * sgt_ref[...].astype(F32)).astype(BF16)


def _fox_attn(qt, ct, ka, vt, sgt, heads, head_dim):
    bsz, fw, s = qt.shape
    tq, tk = ATTN_TQ, ATTN_TK
    n_kv = s // tk
    kern = functools.partial(_attn_kernel, tq=tq, tk=tk, head_dim=head_dim, n_kv=n_kv)
    hps = ATTN_HEADS
    assert heads % hps == 0
    q_block = pl.BlockSpec((None, hps * head_dim, tq), lambda b, h, q: (b, h, q))
    return pl.pallas_call(
        kern,
        grid=(bsz, heads // hps, s // tq),
        in_specs=[
            q_block,
            pl.BlockSpec((None, heads, tq), lambda b, h, q: (b, 0, q)),
            pl.BlockSpec((None, hps, s, LANES), lambda b, h, q: (b, h, 0, 0)),
            pl.BlockSpec((None, hps * head_dim, s), lambda b, h, q: (b, h, 0)),
            q_block,
        ],
        out_specs=q_block,
        out_shape=jax.ShapeDtypeStruct((bsz, fw, s), BF16),
        scratch_shapes=[
            pltpu.VMEM((hps, LANES, tq), BF16),
            pltpu.VMEM((hps, n_kv, head_dim + BF16_ROWS, tk), BF16),
            pltpu.VMEM((hps, 2, tk, tq), F32),
            pltpu.VMEM((hps, head_dim + BF16_ROWS, tq), F32),
        ],
        compiler_params=pltpu.CompilerParams(
            dimension_semantics=("arbitrary", "arbitrary", "arbitrary"),
            vmem_limit_bytes=VMEM_LIMIT_BYTES),
        name="fox_attn",
    )(qt, ct, ka, vt, sgt)


def _fox_out_kernel(yt_ref, x_ref, wo_ref, g_ref, o_ref):
    z = x_ref[...] + lax.dot_general(yt_ref[...], wo_ref[...], (((0,), (0,)), ((), ())),
                                     preferred_element_type=F32)
    o_ref[...] = _rmsnorm(z, g_ref[...])


def _fox_out(yt, x, wo, g):
    bsz, s, d = x.shape
    fw = yt.shape[1]
    tile = OUT_TILE
    return pl.pallas_call(
        _fox_out_kernel,
        grid=(bsz, s // tile),
        in_specs=[
            pl.BlockSpec((None, fw, tile), lambda b, t: (b, 0, t)),
            pl.BlockSpec((None, tile, d), lambda b, t: (b, t, 0)),
            pl.BlockSpec((fw, d), lambda b, t: (0, 0)),
            pl.BlockSpec((1, d), lambda b, t: (0, 0)),
        ],
        out_specs=pl.BlockSpec((None, tile, d), lambda b, t: (b, t, 0)),
        out_shape=jax.ShapeDtypeStruct((bsz, s, d), F32),
        compiler_params=pltpu.CompilerParams(
            dimension_semantics=("arbitrary", "arbitrary"), vmem_limit_bytes=VMEM_LIMIT_BYTES),
        name="fox_out",
    )(yt, x, wo, g)


def _key_aug_constants(heads, head_dim):
    h = jnp.arange(LANES)[:, None]
    lane = jnp.arange(LANES)[None, :]
    sel = jnp.concatenate([jnp.where((h < heads) & (lane == head_dim + N_SPLIT * h + k), -1.0, 0.0)
                           for k in range(N_SPLIT)], axis=0)
    ones = jnp.where((lane >= C_LANE) & (lane < C_LANE + N_SPLIT), 1.0, 0.0)
    return sel.astype(BF16), ones.astype(F32)


def kernel(x, norm_g, final_g, lru_w_in, lru_conv_w, lru_conv_b, lru_wa, lru_ba, lru_wx, lru_bx,
           lru_a_param, lru_w_out, fox_w_in, fox_b_f, fox_w_out):
    assert norm_g.shape[0] == 2 and lru_w_in.shape[0] == 1 and fox_w_in.shape[0] == 1
    d = x.shape[-1]
    width = lru_w_out.shape[1]
    n_blocks, blk = lru_wa.shape[1], lru_wa.shape[2]
    heads = fox_b_f.shape[1]
    fw = fox_w_out.shape[1]
    head_dim = fw // heads
    assert 2 * head_dim == LANES and heads % 2 == 0
    assert head_dim + N_SPLIT * heads <= C_LANE and C_LANE + N_SPLIT <= LANES

    wg = (0.5 * jnp.concatenate([lru_wa[0], lru_wx[0]], axis=-1)).astype(BF16)
    bg = 0.5 * jnp.concatenate([lru_ba[0].reshape(n_blocks, 1, blk), lru_bx[0].reshape(n_blocks, 1, blk)],
                               axis=-1)
    x1 = _lru_layer(x, norm_g[0:1], lru_w_in[0].astype(BF16), lru_conv_w[0], lru_conv_b[0:1], wg, bg,
                    lru_a_param[0:1], lru_w_out[0].astype(BF16), n_blocks)

    w = fox_w_in[0]
    wqt = w[:, 0 * fw:1 * fw].T.astype(BF16)
    wk = w[:, 1 * fw:2 * fw].astype(BF16)
    wvt = w[:, 2 * fw:3 * fw].T.astype(BF16)
    wgt = w[:, 3 * fw:4 * fw].T.astype(BF16)
    wft = jnp.pad(w[:, 4 * fw:].T, ((0, LANES - heads), (0, 0))).astype(BF16)
    bft = jnp.broadcast_to(jnp.pad(fox_b_f[0], (0, LANES - heads))[:, None], (LANES, LANES))
    sel, ones = _key_aug_constants(heads, head_dim)
    ka, qt, vt, sgt, ct = _fox_in(x1, norm_g[1:2], wqt, wk, wvt, wgt, wft, bft, sel, ones,
                                  heads, head_dim)
    yt = _fox_attn(qt, ct, ka, vt, sgt, heads, head_dim)
    return _fox_out(yt, x1, fox_w_out[0].astype(BF16), final_g.reshape(1, d))
```

```python
import functools

import jax
import jax.numpy as jnp
from jax import lax
from jax.experimental import pallas as pl
from jax.experimental.pallas import tpu as pltpu

F32 = jnp.float32
BF16 = jnp.bfloat16

EPS = 1e-6
LRU_C = 8.0
CONV_WIDTH = 4
NEG_INF = -1e30
LOG2E = 1.4426950408889634

LANES = 128
SUBLANES = 8
BF16_ROWS = 16
VMEM_LIMIT_BYTES = 56 * 1024 * 1024

LRU_TILE = 512
FOX_TILE = 512
OUT_TILE = 1024
ATTN_TQ = 2048
ATTN_TK = 256
ATTN_HEADS = 4
SCRATCH_SETS = 2
ATTN_COLS = 512

N_SPLIT = 3
C_LANE = 112


def _rmsnorm(x, g):
    return x * lax.rsqrt(jnp.mean(x * x, axis=-1, keepdims=True) + EPS) * g


def _silu(x):
    h = 0.5 * x
    return h + h * jnp.tanh(h)


def _log_sigmoid(x):
    return jnp.minimum(x, 0.0) - jnp.log1p(jnp.exp(-jnp.abs(x)))


def _split3(c):
    c1 = c.astype(BF16)
    r1 = c - c1.astype(F32)
    c2 = r1.astype(BF16)
    c3 = (r1 - c2.astype(F32)).astype(BF16)
    return c1, c2, c3


def _dot(a, b):
    return jnp.dot(a, b, preferred_element_type=F32)


def _dot_nt(a, b):
    return lax.dot_general(a, b, (((1,), (1,)), ((), ())), preferred_element_type=F32)


def _lru_kernel(x_ref, g_ref, win_ref, cw_ref, cb_ref, wg_ref, bg_ref, ap_ref, wout_ref, o_ref,
                ext_ref, a_ref, b_ref, hs_ref, hcar_ref, *, tile, width, n_blocks):
    t = pl.program_id(1)
    blk = width // n_blocks

    @pl.when(t == 0)
    def _():
        ext_ref[0:SUBLANES, :] = jnp.zeros((SUBLANES, width), F32)
        hcar_ref[...] = jnp.zeros_like(hcar_ref)

    x = x_ref[...]
    xn = _rmsnorm(x, g_ref[...]).astype(BF16)
    u = _dot(xn, win_ref[...])
    gate = u[:, width:]
    ext_ref[SUBLANES:SUBLANES + tile, :] = u[:, :width]

    cw = cw_ref[...]
    xc = cb_ref[...] + cw[3:4] * ext_ref[SUBLANES:SUBLANES + tile, :]
    for k in range(CONV_WIDTH - 1):
        off = SUBLANES - (CONV_WIDTH - 1) + k
        xc = xc + cw[k:k + 1] * ext_ref[off:off + tile, :]
    ext_ref[0:SUBLANES, :] = ext_ref[tile:tile + SUBLANES, :]

    ap = ap_ref[...]
    sp = jnp.maximum(-ap, 0.0) + jnp.log1p(jnp.exp(-jnp.abs(ap)))
    xcb = xc.astype(BF16)
    half_c_sp = (0.5 * LRU_C) * sp
    for n in range(n_blocks):
        sl = slice(n * blk, (n + 1) * blk)
        g = _dot(xcb[:, sl], wg_ref[n]) + bg_ref[n]
        tr = jnp.tanh(g[:, :blk])
        i = 0.5 + 0.5 * jnp.tanh(g[:, blk:])
        neg_log_a = half_c_sp[:, sl] + half_c_sp[:, sl] * tr
        a = jnp.exp(-neg_log_a)
        a_ref[:, sl] = a
        one_minus_a2 = jnp.tanh(neg_log_a) * (1.0 + a * a)
        mult = jnp.where(one_minus_a2 > 0.0, one_minus_a2 * lax.rsqrt(one_minus_a2), 0.0)
        b_ref[:, sl] = mult * (i * xc[:, sl])


    def step(i, h):
        h = a_ref[pl.ds(i, 1), :] * h + b_ref[pl.ds(i, 1), :]
        hs_ref[pl.ds(i, 1), :] = h
        return h

    h = lax.fori_loop(0, tile, step, hcar_ref[0:1, :], unroll=SUBLANES)
    hcar_ref[0:1, :] = h

    y = hs_ref[...] * _silu(gate)
    o_ref[...] = x + _dot(y.astype(BF16), wout_ref[...])


def _lru_layer(x, g, w_in, conv_w, conv_b, wg, bg, a_param, w_out, n_blocks):
    bsz, s, d = x.shape
    width = w_out.shape[0]
    tile = LRU_TILE
    kern = functools.partial(_lru_kernel, tile=tile, width=width, n_blocks=n_blocks)
    const = lambda *shape: pl.BlockSpec(shape, lambda b, t: (0,) * len(shape))
    return pl.pallas_call(
        kern,
        grid=(bsz, s // tile),
        in_specs=[
            pl.BlockSpec((None, tile, d), lambda b, t: (b, t, 0)),
            const(1, d),
            const(d, 2 * width),
            const(CONV_WIDTH, width),
            const(1, width),
            const(n_blocks, width // n_blocks, 2 * (width // n_blocks)),
            const(n_blocks, 1, 2 * (width // n_blocks)),
            const(1, width),
            const(width, d),
        ],
        out_specs=pl.BlockSpec((None, tile, d), lambda b, t: (b, t, 0)),
        out_shape=jax.ShapeDtypeStruct((bsz, s, d), F32),
        scratch_shapes=[
            pltpu.VMEM((tile + SUBLANES, width), F32),
            pltpu.VMEM((tile, width), F32),
            pltpu.VMEM((tile, width), F32),
            pltpu.VMEM((tile, width), F32),
            pltpu.VMEM((SUBLANES, width), F32),
        ],
        compiler_params=pltpu.CompilerParams(
            dimension_semantics=("arbitrary", "arbitrary"), vmem_limit_bytes=VMEM_LIMIT_BYTES),
        name="lru_layer",
    )(x, g, w_in, conv_w, conv_b, wg, bg, a_param, w_out)


def _fox_in_kernel(x_ref, g_ref, wqt_ref, wk_ref, wvt_ref, wgt_ref, wft_ref, bft_ref,
                   sel_ref, ones_ref, ka_ref, qt_ref, vt_ref, sgt_ref, ct_ref, cart_ref,
                   *, tile, heads, head_dim, scale):
    t = pl.program_id(1)

    @pl.when(t == 0)
    def _():
        cart_ref[...] = jnp.zeros_like(cart_ref)

    hn = _rmsnorm(x_ref[...], g_ref[...]).astype(BF16)

    qt_ref[...] = (_dot_nt(wqt_ref[...], hn) * scale).astype(BF16)
    vt_ref[...] = _dot_nt(wvt_ref[...], hn).astype(BF16)
    gt = _dot_nt(wgt_ref[...], hn)
    sgt_ref[...] = _silu(gt).astype(BF16)

    row = lax.broadcasted_iota(jnp.int32, (tile, tile), 0)
    col = lax.broadcasted_iota(jnp.int32, (tile, tile), 1)
    tri_upper = (row <= col).astype(BF16)

    lft = _log_sigmoid(_dot_nt(wft_ref[...], hn) + bft_ref[...][:, 0:1])
    hrow = lax.broadcasted_iota(jnp.int32, lft.shape, 0)
    lft = jnp.where(hrow < heads, lft, 0.0)
    parts = _dot(jnp.concatenate(_split3(lft), axis=0), tri_upper)
    ct = (parts[0:LANES] + parts[LANES:2 * LANES] + parts[2 * LANES:3 * LANES]) + cart_ref[...][:, 0:1]
    cart_ref[...] = jnp.broadcast_to(ct[:, tile - 1:tile], cart_ref.shape)
    ct_ref[...] = ct[0:heads, :]

    c1, c2, c3 = _split3(jnp.transpose(ct) * LOG2E)
    aug = _dot(c1, sel_ref[0]) + _dot(c2, sel_ref[1]) + _dot(c3, sel_ref[2]) + ones_ref[...]

    k = _dot(hn, wk_ref[...])
    lane = lax.broadcasted_iota(jnp.int32, (tile, LANES), 1)
    is_key = lane < head_dim
    for p in range(heads // 2):
        slab = k[:, p * LANES:(p + 1) * LANES]
        ka_ref[2 * p] = jnp.where(is_key, slab, aug).astype(BF16)
        ka_ref[2 * p + 1] = jnp.where(is_key, pltpu.roll(slab, head_dim, 1), aug).astype(BF16)


def _fox_in(x, g, wqt, wk, wvt, wgt, wft, bft, sel, ones, heads, head_dim):
    bsz, s, d = x.shape
    fw = heads * head_dim
    tile = FOX_TILE
    kern = functools.partial(_fox_in_kernel, tile=tile, heads=heads, head_dim=head_dim,
                             scale=LOG2E / (head_dim ** 0.5))
    const = lambda *shape: pl.BlockSpec(shape, lambda b, t: (0,) * len(shape))
    feat_major = pl.BlockSpec((None, fw, tile), lambda b, t: (b, 0, t))
    return pl.pallas_call(
        kern,
        grid=(bsz, s // tile),
        in_specs=[
            pl.BlockSpec((None, tile, d), lambda b, t: (b, t, 0)),
            const(1, d),
            const(fw, d), const(d, fw), const(fw, d), const(fw, d),
            const(LANES, d), const(LANES, LANES),
            const(3, LANES, LANES), const(1, LANES),
        ],
        out_specs=[
            pl.BlockSpec((None, heads, tile, LANES), lambda b, t: (b, 0, t, 0)),
            feat_major, feat_major, feat_major,
            pl.BlockSpec((None, heads, tile), lambda b, t: (b, 0, t)),
        ],
        out_shape=[
            jax.ShapeDtypeStruct((bsz, heads, s, LANES), BF16),
            jax.ShapeDtypeStruct((bsz, fw, s), BF16),
            jax.ShapeDtypeStruct((bsz, fw, s), BF16),
            jax.ShapeDtypeStruct((bsz, fw, s), BF16),
            jax.ShapeDtypeStruct((bsz, heads, s), F32),
        ],
        scratch_shapes=[
            pltpu.VMEM((LANES, LANES), F32),
        ],
        compiler_params=pltpu.CompilerParams(
            dimension_semantics=("arbitrary", "arbitrary"), vmem_limit_bytes=VMEM_LIMIT_BYTES),
        name="fox_in",
    )(x, g, wqt, wk, wvt, wgt, wft, bft, sel, ones)


def _attn_kernel(qt_ref, ct_ref, ka_ref, vt_ref, sgt_ref, yt_ref, qa_ref, vx_ref, s_ref, acc_ref,
                 *, head_dim, **static):
    qi = pl.program_id(2)
    tk, n_kv = static["tk"], static["n_kv"]

    @pl.when(qi == 0)
    def _():
        for hh in range(ATTN_HEADS):
            for j in range(n_kv):
                vx_ref[hh, j, 0:head_dim, :] = vt_ref[hh * head_dim:(hh + 1) * head_dim, j * tk:(j + 1) * tk]
                vx_ref[hh, j, head_dim:, :] = jnp.ones((vx_ref.shape[2] - head_dim, tk), BF16)

    for hh in range(ATTN_HEADS):
        rows = slice(hh * head_dim, (hh + 1) * head_dim)
        st = hh % SCRATCH_SETS
        _attn_head(pl.program_id(1) * ATTN_HEADS + hh, qi, qt_ref.at[rows], ct_ref, ka_ref.at[hh],
                   sgt_ref.at[rows], yt_ref.at[rows], qa_ref.at[st], vx_ref.at[hh],
                   s_ref.at[st, 0], s_ref.at[st, 1], acc_ref.at[st], head_dim=head_dim, **static)


def _attn_head(h, qi, qt_ref, ct_ref, ka_ref, sgt_ref, yt_ref, qa_ref, vx_ref, s0_ref, s1_ref,
               acc_ref, *, tq, tk, head_dim, n_kv):
    n_sub = tq // tk
    assert n_sub * tk == tq and n_sub % 2 == 0

    qa_ref[0:head_dim, :] = qt_ref[...]
    c1, c2, c3 = _split3(ct_ref[pl.ds(h, 1), :] * LOG2E)
    arow = lax.broadcasted_iota(jnp.int32, (LANES - head_dim, tq), 0)
    aug = jnp.where((arow >= N_SPLIT * h) & (arow < N_SPLIT * (h + 1)), 1.0, 0.0)
    cbase = C_LANE - head_dim
    aug = jnp.where(arow == cbase, c1.astype(F32), aug)
    aug = jnp.where(arow == cbase + 1, c2.astype(F32), aug)
    aug = jnp.where(arow == cbase + 2, c3.astype(F32), aug)
    qa_ref[head_dim:, :] = aug.astype(BF16)

    def scores(j, qlo=0):
        start = pl.multiple_of(j * tk, tk)
        return _dot(ka_ref[pl.ds(start, tk), :], qa_ref[:, qlo:])

    def update(s_ref, j, m_all, qlo=None):
        lo = qlo or 0
        m_parts = [m_all[:, :lo]] if lo else []
        for c0 in range(lo, tq, ATTN_COLS):
            c1 = min(c0 + ATTN_COLS, tq)
            cs = c0 - lo
            s = s_ref[:, cs:c1 - lo]
            if qlo is not None and cs < tk:
                kr = lax.broadcasted_iota(jnp.int32, s.shape, 0)
                qc = lax.broadcasted_iota(jnp.int32, s.shape, 1) + cs
                s = jnp.where(kr <= qc, s, NEG_INF)
            m = m_all[:, c0:c1]
            m_new = jnp.maximum(m, jnp.max(s, axis=0, keepdims=True))
            alpha = jnp.exp2(m - m_new)
            p = jnp.exp2(s - m_new).astype(BF16)
            acc_ref[:, c0:c1] = acc_ref[:, c0:c1] * alpha + _dot(vx_ref[j], p)
            m_parts.append(m_new)
        return jnp.concatenate(m_parts, axis=1)

    s_refs = (s0_ref, s1_ref)

    def group(i, carry, diagonal):
        for u in range(n_sub):
            j = n_sub * i + u
            if not (diagonal and u == n_sub - 1):
                qlo_next = (u + 1) * tk if diagonal else 0
                s_refs[(u + 1) % 2][:, 0:tq - qlo_next] = scores(j + 1, qlo_next)
            qlo = u * tk if diagonal else 0
            carry = update(s_refs[u % 2], j, carry, qlo=qlo if diagonal else None)
        return carry

    m0 = jnp.full((1, tq), NEG_INF, F32)
    acc_ref[...] = jnp.zeros_like(acc_ref)
    s0_ref[...] = scores(0)
    m = lax.fori_loop(0, qi, lambda i, c: group(i, c, False), m0)
    group(qi, m, True)

    o = acc_ref[0:head_dim, :] / acc_ref[head_dim:head_dim + 1, :]
    yt_ref[...] = (o * sgt_ref[...].astype(F32)).astype(BF16)


def _fox_attn(qt, ct, ka, vt, sgt, heads, head_dim):
    bsz, fw, s = qt.shape
    tq, tk = ATTN_TQ, ATTN_TK
    n_kv = s // tk
    kern = functools.partial(_attn_kernel, tq=tq, tk=tk, head_dim=head_dim, n_kv=n_kv)
    hps = ATTN_HEADS
    assert heads % hps == 0
    q_block = pl.BlockSpec((None, hps * head_dim, tq), lambda b, h, q: (b, h, q))
    return pl.pallas_call(
        kern,
        grid=(bsz, heads // hps, s // tq),
        in_specs=[
            q_block,
            pl.BlockSpec((None, heads, tq), lambda b, h, q: (b, 0, q)),
            pl.BlockSpec((None, hps, s, LANES), lambda b, h, q: (b, h, 0, 0)),
            pl.BlockSpec((None, hps * head_dim, s), lambda b, h, q: (b, h, 0)),
            q_block,
        ],
        out_specs=q_block,
        out_shape=jax.ShapeDtypeStruct((bsz, fw, s), BF16),
        scratch_shapes=[
            pltpu.VMEM((SCRATCH_SETS, LANES, tq), BF16),
            pltpu.VMEM((hps, n_kv, head_dim + BF16_ROWS, tk), BF16),
            pltpu.VMEM((SCRATCH_SETS, 2, tk, tq), F32),
            pltpu.VMEM((SCRATCH_SETS, head_dim + BF16_ROWS, tq), F32),
        ],
        compiler_params=pltpu.CompilerParams(
            dimension_semantics=("arbitrary", "arbitrary", "arbitrary"),
            vmem_limit_bytes=VMEM_LIMIT_BYTES),
        name="fox_attn",
    )(qt, ct, ka, vt, sgt)


def _fox_out_kernel(yt_ref, x_ref, wo_ref, g_ref, o_ref):
    z = x_ref[...] + lax.dot_general(yt_ref[...], wo_ref[...], (((0,), (0,)), ((), ())),
                                     preferred_element_type=F32)
    o_ref[...] = _rmsnorm(z, g_ref[...])


def _fox_out(yt, x, wo, g):
    bsz, s, d = x.shape
    fw = yt.shape[1]
    tile = OUT_TILE
    return pl.pallas_call(
        _fox_out_kernel,
        grid=(bsz, s // tile),
        in_specs=[
            pl.BlockSpec((None, fw, tile), lambda b, t: (b, 0, t)),
            pl.BlockSpec((None, tile, d), lambda b, t: (b, t, 0)),
            pl.BlockSpec((fw, d), lambda b, t: (0, 0)),
            pl.BlockSpec((1, d), lambda b, t: (0, 0)),
        ],
        out_specs=pl.BlockSpec((None, tile, d), lambda b, t: (b, t, 0)),
        out_shape=jax.ShapeDtypeStruct((bsz, s, d), F32),
        compiler_params=pltpu.CompilerParams(
            dimension_semantics=("arbitrary", "arbitrary"), vmem_limit_bytes=VMEM_LIMIT_BYTES),
        name="fox_out",
    )(yt, x, wo, g)


def _key_aug_constants(heads, head_dim):
    h = jnp.arange(LANES)[:, None]
    lane = jnp.arange(LANES)[None, :]
    sel = jnp.stack([jnp.where((h < heads) & (lane == head_dim + N_SPLIT * h + k), -1.0, 0.0)
                     for k in range(N_SPLIT)])
    ones = jnp.where((lane >= C_LANE) & (lane < C_LANE + N_SPLIT), 1.0, 0.0)
    return sel.astype(BF16), ones.astype(F32)


def kernel(x, norm_g, final_g, lru_w_in, lru_conv_w, lru_conv_b, lru_wa, lru_ba, lru_wx, lru_bx,
           lru_a_param, lru_w_out, fox_w_in, fox_b_f, fox_w_out):
    assert norm_g.shape[0] == 2 and lru_w_in.shape[0] == 1 and fox_w_in.shape[0] == 1
    d = x.shape[-1]
    width = lru_w_out.shape[1]
    n_blocks, blk = lru_wa.shape[1], lru_wa.shape[2]
    heads = fox_b_f.shape[1]
    fw = fox_w_out.shape[1]
    head_dim = fw // heads
    assert 2 * head_dim == LANES and heads % 2 == 0
    assert head_dim + N_SPLIT * heads <= C_LANE and C_LANE + N_SPLIT <= LANES

    wg = (0.5 * jnp.concatenate([lru_wa[0], lru_wx[0]], axis=-1)).astype(BF16)
    bg = 0.5 * jnp.concatenate([lru_ba[0].reshape(n_blocks, 1, blk), lru_bx[0].reshape(n_blocks, 1, blk)],
                               axis=-1)
    x1 = _lru_layer(x, norm_g[0:1], lru_w_in[0].astype(BF16), lru_conv_w[0], lru_conv_b[0:1], wg, bg,
                    lru_a_param[0:1], lru_w_out[0].astype(BF16), n_blocks)

    w = fox_w_in[0]
    wqt = w[:, 0 * fw:1 * fw].T.astype(BF16)
    wk = w[:, 1 * fw:2 * fw].astype(BF16)
    wvt = w[:, 2 * fw:3 * fw].T.astype(BF16)
    wgt = w[:, 3 * fw:4 * fw].T.astype(BF16)
    wft = jnp.pad(w[:, 4 * fw:].T, ((0, LANES - heads), (0, 0))).astype(BF16)
    bft = jnp.broadcast_to(jnp.pad(fox_b_f[0], (0, LANES - heads))[:, None], (LANES, LANES))
    sel, ones = _key_aug_constants(heads, head_dim)
    ka, qt, vt, sgt, ct = _fox_in(x1, norm_g[1:2], wqt, wk, wvt, wgt, wft, bft, sel, ones,
                                  heads, head_dim)
    yt = _fox_attn(qt, ct, ka, vt, sgt, heads, head_dim)
    return _fox_out(yt, x1, fox_w_out[0].astype(BF16), final_g.reshape(1, d))
```

```python
import functools

import jax
import jax.numpy as jnp
from jax import lax
from jax.experimental import pallas as pl
from jax.experimental.pallas import tpu as pltpu

F32 = jnp.float32
BF16 = jnp.bfloat16

EPS = 1e-6
LRU_C = 8.0
CONV_WIDTH = 4
NEG_INF = -1e30
LOG2E = 1.4426950408889634

LANES = 128
SUBLANES = 8
BF16_ROWS = 16
VMEM_LIMIT_BYTES = 56 * 1024 * 1024

LRU_TILE = 512
FOX_TILE = 512
OUT_TILE = 1024
ATTN_TQ = 2048
ATTN_TK = 256
ATTN_HEADS = 4
SCRATCH_SETS = 2
ATTN_COLS = 512

N_SPLIT = 3
C_LANE = 112


def _rmsnorm(x, g):
    return x * lax.rsqrt(jnp.mean(x * x, axis=-1, keepdims=True) + EPS) * g


def _silu(x):
    h = 0.5 * x
    return h + h * jnp.tanh(h)


def _log_sigmoid(x):
    return jnp.minimum(x, 0.0) - jnp.log1p(jnp.exp(-jnp.abs(x)))


def _split3(c):
    c1 = c.astype(BF16)
    r1 = c - c1.astype(F32)
    c2 = r1.astype(BF16)
    c3 = (r1 - c2.astype(F32)).astype(BF16)
    return c1, c2, c3


def _dot(a, b):
    return jnp.dot(a, b, preferred_element_type=F32)


def _dot_nt(a, b):
    return lax.dot_general(a, b, (((1,), (1,)), ((), ())), preferred_element_type=F32)


def _lru_kernel(x_ref, g_ref, win_ref, cw_ref, cb_ref, wg_ref, bg_ref, ap_ref, wout_ref, o_ref,
                ext_ref, a_ref, b_ref, hs_ref, hcar_ref, *, tile, width, n_blocks):
    t = pl.program_id(1)
    blk = width // n_blocks

    @pl.when(t == 0)
    def _():
        ext_ref[0:SUBLANES, :] = jnp.zeros((SUBLANES, width), F32)
        hcar_ref[...] = jnp.zeros_like(hcar_ref)

    x = x_ref[...]
    xn = _rmsnorm(x, g_ref[...]).astype(BF16)
    u = _dot(xn, win_ref[...])
    gate = u[:, width:]
    ext_ref[SUBLANES:SUBLANES + tile, :] = u[:, :width]

    cw = cw_ref[...]
    xc = cb_ref[...] + cw[3:4] * ext_ref[SUBLANES:SUBLANES + tile, :]
    for k in range(CONV_WIDTH - 1):
        off = SUBLANES - (CONV_WIDTH - 1) + k
        xc = xc + cw[k:k + 1] * ext_ref[off:off + tile, :]
    ext_ref[0:SUBLANES, :] = ext_ref[tile:tile + SUBLANES, :]

    ap = ap_ref[...]
    sp = jnp.maximum(-ap, 0.0) + jnp.log1p(jnp.exp(-jnp.abs(ap)))
    xcb = xc.astype(BF16)
    half_c_sp = (0.5 * LRU_C) * sp
    for n in range(n_blocks):
        sl = slice(n * blk, (n + 1) * blk)
        g = _dot(xcb[:, sl], wg_ref[n]) + bg_ref[n]
        tr = jnp.tanh(g[:, :blk])
        i = 0.5 + 0.5 * jnp.tanh(g[:, blk:])
        neg_log_a = half_c_sp[:, sl] + half_c_sp[:, sl] * tr
        a = jnp.exp(-neg_log_a)
        a_ref[:, sl] = a
        one_minus_a2 = jnp.tanh(neg_log_a) * (1.0 + a * a)
        mult = jnp.where(one_minus_a2 > 0.0, one_minus_a2 * lax.rsqrt(one_minus_a2), 0.0)
        b_ref[:, sl] = mult * (i * xc[:, sl])


    def step(i, h):
        h = a_ref[pl.ds(i, 1), :] * h + b_ref[pl.ds(i, 1), :]
        hs_ref[pl.ds(i, 1), :] = h
        return h

    h = lax.fori_loop(0, tile, step, hcar_ref[0:1, :], unroll=SUBLANES)
    hcar_ref[0:1, :] = h

    y = hs_ref[...] * _silu(gate)
    o_ref[...] = x + _dot(y.astype(BF16), wout_ref[...])


def _lru_layer(x, g, w_in, conv_w, conv_b, wg, bg, a_param, w_out, n_blocks):
    bsz, s, d = x.shape
    width = w_out.shape[0]
    tile = LRU_TILE
    kern = functools.partial(_lru_kernel, tile=tile, width=width, n_blocks=n_blocks)
    const = lambda *shape: pl.BlockSpec(shape, lambda b, t: (0,) * len(shape))
    return pl.pallas_call(
        kern,
        grid=(bsz, s // tile),
        in_specs=[
            pl.BlockSpec((None, tile, d), lambda b, t: (b, t, 0)),
            const(1, d),
            const(d, 2 * width),
            const(CONV_WIDTH, width),
            const(1, width),
            const(n_blocks, width // n_blocks, 2 * (width // n_blocks)),
            const(n_blocks, 1, 2 * (width // n_blocks)),
            const(1, width),
            const(width, d),
        ],
        out_specs=pl.BlockSpec((None, tile, d), lambda b, t: (b, t, 0)),
        out_shape=jax.ShapeDtypeStruct((bsz, s, d), F32),
        scratch_shapes=[
            pltpu.VMEM((tile + SUBLANES, width), F32),
            pltpu.VMEM((tile, width), F32),
            pltpu.VMEM((tile, width), F32),
            pltpu.VMEM((tile, width), F32),
            pltpu.VMEM((SUBLANES, width), F32),
        ],
        compiler_params=pltpu.CompilerParams(
            dimension_semantics=("arbitrary", "arbitrary"), vmem_limit_bytes=VMEM_LIMIT_BYTES,
            allow_input_fusion=[False, False, True, False, False, True, True, False, True]),
        name="lru_layer",
    )(x, g, w_in, conv_w, conv_b, wg, bg, a_param, w_out)


def _fox_in_kernel(x_ref, g_ref, wqt_ref, wk_ref, wvt_ref, wgt_ref, wft_ref, bft_ref,
                   sel_ref, ones_ref, ka_ref, qt_ref, vt_ref, sgt_ref, ct_ref, cart_ref,
                   *, tile, heads, head_dim, scale):
    t = pl.program_id(1)

    @pl.when(t == 0)
    def _():
        cart_ref[...] = jnp.zeros_like(cart_ref)

    hn = _rmsnorm(x_ref[...], g_ref[...]).astype(BF16)

    qt_ref[...] = (_dot_nt(wqt_ref[...], hn) * scale).astype(BF16)
    vt_ref[...] = _dot_nt(wvt_ref[...], hn).astype(BF16)
    gt = _dot_nt(wgt_ref[...], hn)
    sgt_ref[...] = _silu(gt).astype(BF16)

    row = lax.broadcasted_iota(jnp.int32, (tile, tile), 0)
    col = lax.broadcasted_iota(jnp.int32, (tile, tile), 1)
    tri_upper = (row <= col).astype(BF16)

    lft = _log_sigmoid(_dot_nt(wft_ref[...], hn) + bft_ref[...][:, 0:1])
    hrow = lax.broadcasted_iota(jnp.int32, lft.shape, 0)
    lft = jnp.where(hrow < heads, lft, 0.0)
    parts = _dot(jnp.concatenate(_split3(lft), axis=0), tri_upper)
    ct = (parts[0:LANES] + parts[LANES:2 * LANES] + parts[2 * LANES:3 * LANES]) + cart_ref[...][:, 0:1]
    cart_ref[...] = jnp.broadcast_to(ct[:, tile - 1:tile], cart_ref.shape)
    ct_ref[...] = ct[0:heads, :]

    c1, c2, c3 = _split3(jnp.transpose(ct) * LOG2E)
    aug = _dot(c1, sel_ref[0]) + _dot(c2, sel_ref[1]) + _dot(c3, sel_ref[2]) + ones_ref[...]

    k = _dot(hn, wk_ref[...])
    lane = lax.broadcasted_iota(jnp.int32, (tile, LANES), 1)
    is_key = lane < head_dim
    for p in range(heads // 2):
        slab = k[:, p * LANES:(p + 1) * LANES]
        ka_ref[2 * p] = jnp.where(is_key, slab, aug).astype(BF16)
        ka_ref[2 * p + 1] = jnp.where(is_key, pltpu.roll(slab, head_dim, 1), aug).astype(BF16)


def _fox_in(x, g, wqt, wk, wvt, wgt, wft, bft, sel, ones, heads, head_dim):
    bsz, s, d = x.shape
    fw = heads * head_dim
    tile = FOX_TILE
    kern = functools.partial(_fox_in_kernel, tile=tile, heads=heads, head_dim=head_dim,
                             scale=LOG2E / (head_dim ** 0.5))
    const = lambda *shape: pl.BlockSpec(shape, lambda b, t: (0,) * len(shape))
    feat_major = pl.BlockSpec((None, fw, tile), lambda b, t: (b, 0, t))
    return pl.pallas_call(
        kern,
        grid=(bsz, s // tile),
        in_specs=[
            pl.BlockSpec((None, tile, d), lambda b, t: (b, t, 0)),
            const(1, d),
            const(fw, d), const(d, fw), const(fw, d), const(fw, d),
            const(LANES, d), const(LANES, LANES),
            const(3, LANES, LANES), const(1, LANES),
        ],
        out_specs=[
            pl.BlockSpec((None, heads, tile, LANES), lambda b, t: (b, 0, t, 0)),
            feat_major, feat_major, feat_major,
            pl.BlockSpec((None, heads, tile), lambda b, t: (b, 0, t)),
        ],
        out_shape=[
            jax.ShapeDtypeStruct((bsz, heads, s, LANES), BF16),
            jax.ShapeDtypeStruct((bsz, fw, s), BF16),
            jax.ShapeDtypeStruct((bsz, fw, s), BF16),
            jax.ShapeDtypeStruct((bsz, fw, s), BF16),
            jax.ShapeDtypeStruct((bsz, heads, s), F32),
        ],
        scratch_shapes=[
            pltpu.VMEM((LANES, LANES), F32),
        ],
        compiler_params=pltpu.CompilerParams(
            dimension_semantics=("arbitrary", "arbitrary"), vmem_limit_bytes=VMEM_LIMIT_BYTES,
            allow_input_fusion=[False, False, True, True, True, True, True, True, False, False]),
        name="fox_in",
    )(x, g, wqt, wk, wvt, wgt, wft, bft, sel, ones)


def _attn_kernel(qt_ref, ct_ref, ka_ref, vt_ref, sgt_ref, yt_ref, qa_ref, vx_ref, s_ref, acc_ref,
                 *, head_dim, **static):
    qi = pl.program_id(2)
    tk, n_kv = static["tk"], static["n_kv"]

    @pl.when(qi == 0)
    def _():
        for hh in range(ATTN_HEADS):
            for j in range(n_kv):
                vx_ref[hh, j, 0:head_dim, :] = vt_ref[hh * head_dim:(hh + 1) * head_dim, j * tk:(j + 1) * tk]
                vx_ref[hh, j, head_dim:, :] = jnp.ones((vx_ref.shape[2] - head_dim, tk), BF16)

    for hh in range(ATTN_HEADS):
        rows = slice(hh * head_dim, (hh + 1) * head_dim)
        st = hh % SCRATCH_SETS
        _attn_head(pl.program_id(1) * ATTN_HEADS + hh, qi, qt_ref.at[rows], ct_ref, ka_ref.at[hh],
                   sgt_ref.at[rows], yt_ref.at[rows], qa_ref.at[st], vx_ref.at[hh],
                   s_ref.at[st, 0], s_ref.at[st, 1], acc_ref.at[st], head_dim=head_dim, **static)


def _attn_head(h, qi, qt_ref, ct_ref, ka_ref, sgt_ref, yt_ref, qa_ref, vx_ref, s0_ref, s1_ref,
               acc_ref, *, tq, tk, head_dim, n_kv):
    n_sub = tq // tk
    assert n_sub * tk == tq and n_sub % 2 == 0

    qa_ref[0:head_dim, :] = qt_ref[...]
    c1, c2, c3 = _split3(ct_ref[pl.ds(h, 1), :] * LOG2E)
    arow = lax.broadcasted_iota(jnp.int32, (LANES - head_dim, tq), 0)
    aug = jnp.where((arow >= N_SPLIT * h) & (arow < N_SPLIT * (h + 1)), 1.0, 0.0)
    cbase = C_LANE - head_dim
    aug = jnp.where(arow == cbase, c1.astype(F32), aug)
    aug = jnp.where(arow == cbase + 1, c2.astype(F32), aug)
    aug = jnp.where(arow == cbase + 2, c3.astype(F32), aug)
    qa_ref[head_dim:, :] = aug.astype(BF16)

    def scores(j, qlo=0):
        start = pl.multiple_of(j * tk, tk)
        return _dot(ka_ref[pl.ds(start, tk), :], qa_ref[:, qlo:])

    def update(s_ref, j, m_all, qlo=None):
        lo = qlo or 0
        m_parts = [m_all[:, :lo]] if lo else []
        for c0 in range(lo, tq, ATTN_COLS):
            c1 = min(c0 + ATTN_COLS, tq)
            cs = c0 - lo
            s = s_ref[:, cs:c1 - lo]
            if qlo is not None and cs < tk:
                kr = lax.broadcasted_iota(jnp.int32, s.shape, 0)
                qc = lax.broadcasted_iota(jnp.int32, s.shape, 1) + cs
                s = jnp.where(kr <= qc, s, NEG_INF)
            m = m_all[:, c0:c1]
            m_new = jnp.maximum(m, jnp.max(s, axis=0, keepdims=True))
            alpha = jnp.exp2(m - m_new)
            p = jnp.exp2(s - m_new).astype(BF16)
            acc_ref[:, c0:c1] = acc_ref[:, c0:c1] * alpha + _dot(vx_ref[j], p)
            m_parts.append(m_new)
        return jnp.concatenate(m_parts, axis=1)

    s_refs = (s0_ref, s1_ref)

    def group(i, carry, diagonal):
        for u in range(n_sub):
            j = n_sub * i + u
            if not (diagonal and u == n_sub - 1):
                qlo_next = (u + 1) * tk if diagonal else 0
                s_refs[(u + 1) % 2][:, 0:tq - qlo_next] = scores(j + 1, qlo_next)
            qlo = u * tk if diagonal else 0
            carry = update(s_refs[u % 2], j, carry, qlo=qlo if diagonal else None)
        return carry

    m0 = jnp.full((1, tq), NEG_INF, F32)
    acc_ref[...] = jnp.zeros_like(acc_ref)
    s0_ref[...] = scores(0)
    m = lax.fori_loop(0, qi, lambda i, c: group(i, c, False), m0)
    group(qi, m, True)

    o = acc_ref[0:head_dim, :] / acc_ref[head_dim:head_dim + 1, :]
    yt_ref[...] = (o * sgt_ref[...].astype(F32)).astype(BF16)


def _fox_attn(qt, ct, ka, vt, sgt, heads, head_dim):
    bsz, fw, s = qt.shape
    tq, tk = ATTN_TQ, ATTN_TK
    n_kv = s // tk
    kern = functools.partial(_attn_kernel, tq=tq, tk=tk, head_dim=head_dim, n_kv=n_kv)
    hps = ATTN_HEADS
    assert heads % hps == 0
    q_block = pl.BlockSpec((None, hps * head_dim, tq), lambda b, h, q: (b, h, q))
    return pl.pallas_call(
        kern,
        grid=(bsz, heads // hps, s // tq),
        in_specs=[
            q_block,
            pl.BlockSpec((None, heads, tq), lambda b, h, q: (b, 0, q)),
            pl.BlockSpec((None, hps, s, LANES), lambda b, h, q: (b, h, 0, 0)),
            pl.BlockSpec((None, hps * head_dim, s), lambda b, h, q: (b, h, 0)),
            q_block,
        ],
        out_specs=q_block,
        out_shape=jax.ShapeDtypeStruct((bsz, fw, s), BF16),
        scratch_shapes=[
            pltpu.VMEM((SCRATCH_SETS, LANES, tq), BF16),
            pltpu.VMEM((hps, n_kv, head_dim + BF16_ROWS, tk), BF16),
            pltpu.VMEM((SCRATCH_SETS, 2, tk, tq), F32),
            pltpu.VMEM((SCRATCH_SETS, head_dim + BF16_ROWS, tq), F32),
        ],
        compiler_params=pltpu.CompilerParams(
            dimension_semantics=("arbitrary", "arbitrary", "arbitrary"),
            vmem_limit_bytes=VMEM_LIMIT_BYTES),
        name="fox_attn",
    )(qt, ct, ka, vt, sgt)


def _fox_out_kernel(yt_ref, x_ref, wo_ref, g_ref, o_ref):
    z = x_ref[...] + lax.dot_general(yt_ref[...], wo_ref[...], (((0,), (0,)), ((), ())),
                                     preferred_element_type=F32)
    o_ref[...] = _rmsnorm(z, g_ref[...])


def _fox_out(yt, x, wo, g):
    bsz, s, d = x.shape
    fw = yt.shape[1]
    tile = OUT_TILE
    return pl.pallas_call(
        _fox_out_kernel,
        grid=(bsz, s // tile),
        in_specs=[
            pl.BlockSpec((None, fw, tile), lambda b, t: (b, 0, t)),
            pl.BlockSpec((None, tile, d), lambda b, t: (b, t, 0)),
            pl.BlockSpec((fw, d), lambda b, t: (0, 0)),
            pl.BlockSpec((1, d), lambda b, t: (0, 0)),
        ],
        out_specs=pl.BlockSpec((None, tile, d), lambda b, t: (b, t, 0)),
        out_shape=jax.ShapeDtypeStruct((bsz, s, d), F32),
        compiler_params=pltpu.CompilerParams(
            dimension_semantics=("arbitrary", "arbitrary"), vmem_limit_bytes=VMEM_LIMIT_BYTES,
            allow_input_fusion=[False, False, True, False]),
        name="fox_out",
    )(yt, x, wo, g)


def _key_aug_constants(heads, head_dim):
    h = jnp.arange(LANES)[:, None]
    lane = jnp.arange(LANES)[None, :]
    sel = jnp.stack([jnp.where((h < heads) & (lane == head_dim + N_SPLIT * h + k), -1.0, 0.0)
                     for k in range(N_SPLIT)])
    ones = jnp.where((lane >= C_LANE) & (lane < C_LANE + N_SPLIT), 1.0, 0.0)
    return sel.astype(BF16), ones.astype(F32)


def kernel(x, norm_g, final_g, lru_w_in, lru_conv_w, lru_conv_b, lru_wa, lru_ba, lru_wx, lru_bx,
           lru_a_param, lru_w_out, fox_w_in, fox_b_f, fox_w_out):
    assert norm_g.shape[0] == 2 and lru_w_in.shape[0] == 1 and fox_w_in.shape[0] == 1
    d = x.shape[-1]
    width = lru_w_out.shape[1]
    n_blocks, blk = lru_wa.shape[1], lru_wa.shape[2]
    heads = fox_b_f.shape[1]
    fw = fox_w_out.shape[1]
    head_dim = fw // heads
    assert 2 * head_dim == LANES and heads % 2 == 0
    assert head_dim + N_SPLIT * heads <= C_LANE and C_LANE + N_SPLIT <= LANES

    wg = (0.5 * jnp.concatenate([lru_wa[0], lru_wx[0]], axis=-1)).astype(BF16)
    bg = 0.5 * jnp.concatenate([lru_ba[0].reshape(n_blocks, 1, blk), lru_bx[0].reshape(n_blocks, 1, blk)],
                               axis=-1)
    x1 = _lru_layer(x, norm_g[0:1], lru_w_in[0].astype(BF16), lru_conv_w[0], lru_conv_b[0:1], wg, bg,
                    lru_a_param[0:1], lru_w_out[0].astype(BF16), n_blocks)

    w = fox_w_in[0]
    wqt = w[:, 0 * fw:1 * fw].T.astype(BF16)
    wk = w[:, 1 * fw:2 * fw].astype(BF16)
    wvt = w[:, 2 * fw:3 * fw].T.astype(BF16)
    wgt = w[:, 3 * fw:4 * fw].T.astype(BF16)
    wft = jnp.pad(w[:, 4 * fw:].T, ((0, LANES - heads), (0, 0))).astype(BF16)
    bft = jnp.broadcast_to(jnp.pad(fox_b_f[0], (0, LANES - heads))[:, None], (LANES, LANES))
    sel, ones = _key_aug_constants(heads, head_dim)
    ka, qt, vt, sgt, ct = _fox_in(x1, norm_g[1:2], wqt, wk, wvt, wgt, wft, bft, sel, ones,
                                  heads, head_dim)
    yt = _fox_attn(qt, ct, ka, vt, sgt, heads, head_dim)
    return _fox_out(yt, x1, fox_w_out[0].astype(BF16), final_g.reshape(1, d))
```
